```python
import math
import jax, jax.numpy as jnp
from jax import lax
import numpy as np

D_MODEL = 1024
BATCH = 8
SEQ = 4096
DEPTH = 2

CHUNK = 64
Q_BLOCK = 128
CONV_DIM = D_MODEL // 2
CONV_WIDTH = 3
HG_HEADS = 4
HG_DK = 128
HG_DV = (D_MODEL // 2) // HG_HEADS
HG_FDIM = HG_HEADS * HG_DK
HG_WIDTH = HG_HEADS * HG_DV
SB_HEADS = 16
SB_HEAD_DIM = D_MODEL // SB_HEADS
SB_WIDTH = SB_HEADS * SB_HEAD_DIM
D_FF = 4 * D_MODEL
N_EVEN = (DEPTH + 1) // 2
N_ODD = DEPTH // 2
AB_IN = 3 * CONV_DIM + 2 * HG_FDIM + 2 * HG_WIDTH
AB_MIX = CONV_DIM + HG_WIDTH
AB_SPLITS = [CONV_DIM, 2 * CONV_DIM, 3 * CONV_DIM,
             3 * CONV_DIM + HG_FDIM, 3 * CONV_DIM + 2 * HG_FDIM,
             3 * CONV_DIM + 2 * HG_FDIM + HG_WIDTH]
EPS = 1e-6

kernel_name = "hybrid_chunk_causal_conv_hgrn2_stickbreak"


def rms_norm(x, g):
    xf = x.astype(jnp.float32)
    y = xf * lax.rsqrt(jnp.mean(xf * xf, axis=-1, keepdims=True) + EPS)
    return (y * g.astype(jnp.float32)).astype(x.dtype)


def causal_depthwise_conv(u, w):
    width, ch = w.shape
    return lax.conv_general_dilated(
        u, w[:, None, :], window_strides=(1,), padding=[(width - 1, 0)],
        dimension_numbers=('NWC', 'WIO', 'NWC'), feature_group_count=ch)


def hgrn2_chunkwise(q, k, v, log_f):
    B, S, H, Dk = q.shape
    Dv = v.shape[-1]
    n = S // CHUNK

    def to_chunks(t):
        return t.reshape(B, n, CHUNK, H, t.shape[-1]).transpose(1, 0, 3, 2, 4)

    qc, kc, vc, gc = to_chunks(q), to_chunks(k), to_chunks(v), to_chunks(log_f)
    bc = jnp.cumsum(gc, axis=3)
    causal = jnp.tril(jnp.ones((CHUNK, CHUNK), dtype=bool))

    def step(state, inp):
        q_, k_, v_, b_ = inp
        inter = jnp.einsum('bhtk,bhkv->bhtv', q_ * jnp.exp(b_), state)
        diff = jnp.where(causal[None, None, :, :, None],
                         b_[:, :, :, None, :] - b_[:, :, None, :, :], -jnp.inf)
        decay = jnp.exp(diff)
        scores = jnp.einsum('bhtk,bhsk,bhtsk->bhts', q_, k_, decay)
        intra = jnp.einsum('bhts,bhsv->bhtv', scores, v_)
        b_last = b_[:, :, -1:, :]
        new_state = (jnp.exp(b_last[:, :, 0, :])[..., None] * state
                     + jnp.einsum('bhsk,bhsv->bhkv', k_ * jnp.exp(b_last - b_), v_))
        return new_state, inter + intra

    init = jnp.zeros((B, H, Dk, Dv), jnp.float32)
    _, out = lax.scan(step, init, (qc, kc, vc, bc))
    return out.transpose(1, 0, 3, 2, 4).reshape(B, S, H, Dv)


def stick_breaking_attention(q, k, v):
    S = q.shape[2]
    scale = SB_HEAD_DIM ** -0.5
    outs = []
    for blk in range(S // Q_BLOCK):
        q0 = blk * Q_BLOCK
        end = q0 + Q_BLOCK
        qb = q[:, :, q0:end]
        kb = k[:, :, :end]
        vb = v[:, :, :end]
        z = jnp.einsum('bhqd,bhkd->bhqk', qb, kb) * scale
        qpos = q0 + jnp.arange(Q_BLOCK)
        kpos = jnp.arange(end)
        mask = (kpos[None, :] < qpos[:, None])[None, None]
        log_beta = jax.nn.log_sigmoid(z)
        log_one_minus = jnp.where(mask, log_beta - z, 0.0)
        later = lax.cumsum(log_one_minus, axis=3, reverse=True) - log_one_minus
        w = jnp.where(mask, jnp.exp(log_beta + later), 0.0)
        outs.append(jnp.einsum('bhqk,bhkd->bhqd', w, vb))
    return jnp.concatenate(outs, axis=2)


def mixer_conv_hgrn(h, w_in, conv_w, hg_norm, lower_bound, w_out):
    B, S, _ = h.shape
    u = h @ w_in
    a_b, a_c, a_h, hq, hf, hi, hg = jnp.split(u, AB_SPLITS, axis=-1)
    y_a = a_b * causal_depthwise_conv(a_c * a_h, conv_w)
    f = lower_bound + (1.0 - lower_bound) * jax.nn.sigmoid(hf.astype(jnp.float32))
    log_f = jnp.log(f)
    k_in = 1.0 - f
    heads_k = lambda t: t.reshape(B, S, HG_HEADS, HG_DK)
    o = hgrn2_chunkwise(heads_k(hq.astype(jnp.float32)), heads_k(k_in),
                        hi.astype(jnp.float32).reshape(B, S, HG_HEADS, HG_DV), heads_k(log_f))
    o = rms_norm(o, hg_norm.reshape(HG_HEADS, HG_DV))
    y_b = (o.reshape(B, S, HG_WIDTH) * jax.nn.silu(hg.astype(jnp.float32))).astype(h.dtype)
    return jnp.concatenate([y_a, y_b], axis=-1) @ w_out


def mixer_stick_breaking(h, w_qkv, q_norm, k_norm, w_out):
    B, S, _ = h.shape
    qkv = (h @ w_qkv).reshape(B, S, 3, SB_HEADS, SB_HEAD_DIM)
    q = rms_norm(qkv[:, :, 0], q_norm)
    k = rms_norm(qkv[:, :, 1], k_norm)
    v = qkv[:, :, 2]
    to_bhsd = lambda t: t.astype(jnp.float32).transpose(0, 2, 1, 3)
    o = stick_breaking_attention(to_bhsd(q), to_bhsd(k), to_bhsd(v))
    o = o.transpose(0, 2, 1, 3).reshape(B, S, SB_WIDTH).astype(h.dtype)
    return o @ w_out


def setup_inputs(seed: int = 0) -> dict:
    key = jax.random.key(seed)
    ks = jax.random.split(key, 20)
    nrm = lambda k, shape, s: jax.random.normal(k, shape, jnp.float32) * s
    gain = lambda k, shape: 1.0 + 0.02 * jax.random.normal(k, shape, jnp.float32)
    return {
        "x": nrm(ks[0], (BATCH, SEQ, D_MODEL), 1.0),
        "c": nrm(ks[1], (BATCH, D_MODEL), 1.0),
        "ada_w": nrm(ks[2], (DEPTH, D_MODEL, 6 * D_MODEL), 0.5 * D_MODEL ** -0.5),
        "ada_b": nrm(ks[3], (DEPTH, 6 * D_MODEL), 0.02),
        "norm_mix": gain(ks[4], (DEPTH, D_MODEL)),
        "norm_mlp": gain(ks[5], (DEPTH, D_MODEL)),
        "w_in_ab": nrm(ks[6], (N_EVEN, D_MODEL, AB_IN), D_MODEL ** -0.5),
        "conv_w": nrm(ks[7], (N_EVEN, CONV_WIDTH, CONV_DIM), CONV_WIDTH ** -0.5),
        "hg_norm": gain(ks[8], (N_EVEN, HG_WIDTH)),
        "lb_logits": nrm(ks[9], (DEPTH + 1, HG_FDIM), 0.1),
        "w_out_ab": nrm(ks[10], (N_EVEN, AB_MIX, D_MODEL), AB_MIX ** -0.5),
        "w_qkv": nrm(ks[11], (N_ODD, D_MODEL, 3 * SB_WIDTH), D_MODEL ** -0.5),
        "q_norm": gain(ks[12], (N_ODD, SB_HEAD_DIM)),
        "k_norm": gain(ks[13], (N_ODD, SB_HEAD_DIM)),
        "w_out_c": nrm(ks[14], (N_ODD, SB_WIDTH, D_MODEL), SB_WIDTH ** -0.5),
        "mlp_w1": nrm(ks[15], (DEPTH, D_MODEL, D_FF), D_MODEL ** -0.5),
        "mlp_w2": nrm(ks[16], (DEPTH, D_FF, D_MODEL), D_FF ** -0.5),
    }


def reference(x, c, ada_w, ada_b, norm_mix, norm_mlp, w_in_ab, conv_w, hg_norm,
              lb_logits, w_out_ab, w_qkv, q_norm, k_norm, w_out_c, mlp_w1, mlp_w2):
    c_act = jax.nn.silu(c)
    lower_bounds = jnp.cumsum(jax.nn.softmax(lb_logits.astype(jnp.float32), axis=0), axis=0)
    for layer in range(DEPTH):
        mod = c_act @ ada_w[layer] + ada_b[layer]
        shift1, scale1, gate1, shift2, scale2, gate2 = jnp.split(mod[:, None, :], 6, axis=-1)
        h = rms_norm(x, norm_mix[layer]) * (1.0 + scale1) + shift1
        j = layer // 2
        if layer % 2 == 0:
            y = mixer_conv_hgrn(h, w_in_ab[j], conv_w[j], hg_norm[j],
                                lower_bounds[layer], w_out_ab[j])
        else:
            y = mixer_stick_breaking(h, w_qkv[j], q_norm[j], k_norm[j], w_out_c[j])
        x = x + gate1 * y
        h = rms_norm(x, norm_mlp[layer]) * (1.0 + scale2) + shift2
        x = x + gate2 * (jnp.square(jax.nn.relu(h @ mlp_w1[layer])) @ mlp_w2[layer])
    return x
```

```python
import functools
import math

import jax
import jax.numpy as jnp
from jax import lax
from jax.experimental import pallas as pl
from jax.experimental.pallas import tpu as pltpu

F32 = jnp.float32
BF16 = jnp.bfloat16
HIGHEST = lax.Precision.HIGHEST

EPS = 1e-6
HG_CHUNK = 64
HG_SUB = 16
HG_HEADS = 4
HG_DK = 128
SB_HEADS = 16
SB_HEAD_DIM = 64
ATT_BLOCK = 256
SUBLANES = 8
LANES = 128
VMEM_LIMIT = 56 * 1024 * 1024


def _cparams(sem):
    return pltpu.CompilerParams(dimension_semantics=sem, vmem_limit_bytes=VMEM_LIMIT)


def _norm_mod(x, g, scale, shift):
    ms = jnp.mean(x * x, axis=-1, keepdims=True)
    return (x * lax.rsqrt(ms + EPS) * g) * (1.0 + scale) + shift


def _ada_kernel(c_ref, w_ref, b_ref, o_ref):
    c = c_ref[...]
    ca = c * jax.nn.sigmoid(c)
    o_ref[0] = jnp.dot(ca, w_ref[0], preferred_element_type=F32, precision=HIGHEST) + b_ref[0]


def _ada(c, ada_w, ada_b):
    depth, d, n = ada_w.shape
    bsz = c.shape[0]
    tn = 1536 if n % 1536 == 0 else n
    return pl.pallas_call(
        _ada_kernel,
        grid=(depth, n // tn),
        in_specs=[
            pl.BlockSpec((bsz, d), lambda l, j: (0, 0)),
            pl.BlockSpec((1, d, tn), lambda l, j: (l, 0, j)),
            pl.BlockSpec((1, 1, tn), lambda l, j: (l, 0, j)),
        ],
        out_specs=pl.BlockSpec((1, bsz, tn), lambda l, j: (l, 0, j)),
        out_shape=jax.ShapeDtypeStruct((depth, bsz, n), F32),
        compiler_params=_cparams(("arbitrary", "arbitrary")),
        name="ada",
    )(c, ada_w, ada_b.reshape(depth, 1, n))


def _l0_kernel(x_ref, mod_ref, g_ref, w_ref, cw_ref, hgn_ref, lb_ref, y_ref,
               hb_scr, p_scr, q_scr, k_scr, v_scr, g_scr, b_scr, o_scr, st_scr,
               *, tm, layer):
    i = pl.program_id(1)
    cdim = cw_ref.shape[1]
    fdim = HG_HEADS * HG_DK

    @pl.when(i == 0)
    def _():
        st_scr[...] = jnp.zeros_like(st_scr)
        p_scr[0:SUBLANES, :] = jnp.zeros((SUBLANES, cdim), F32)

    mod = mod_ref[0, 0]
    h = _norm_mod(x_ref[0], g_ref[...], mod[1:2], mod[0:1])
    hb_scr[...] = h.astype(BF16)

    def proj(j, width):
        return jnp.dot(hb_scr[...], w_ref[:, j:j + width], preferred_element_type=F32)

    a_b = proj(0, cdim)
    p = proj(cdim, cdim) * proj(2 * cdim, cdim)
    p_scr[SUBLANES:SUBLANES + tm, :] = p
    p1 = p_scr[SUBLANES - 1:SUBLANES - 1 + tm, :]
    p2 = p_scr[SUBLANES - 2:SUBLANES - 2 + tm, :]
    cw = cw_ref[...]
    y_a = a_b * (cw[0:1] * p2 + cw[1:2] * p1 + cw[2:3] * p)
    p_scr[0:SUBLANES, :] = p_scr[tm:tm + SUBLANES, :]
    y_ref[0, :, 0:cdim] = y_a.astype(y_ref.dtype)

    base = 3 * cdim
    lbl = lb_ref[...]
    e = jnp.exp(lbl - jnp.max(lbl, axis=0, keepdims=True))
    lb = jnp.sum(e[0:layer + 1], axis=0, keepdims=True) / jnp.sum(e, axis=0, keepdims=True)
    q_scr[...] = proj(base, fdim)
    f = lb + (1.0 - lb) * jax.nn.sigmoid(proj(base + fdim, fdim))
    g_scr[...] = jnp.log(f)
    k_scr[...] = 1.0 - f
    v_scr[...] = proj(base + 2 * fdim, fdim)

    row = lax.broadcasted_iota(jnp.int32, (HG_CHUNK, HG_CHUNK), 0)
    col = lax.broadcasted_iota(jnp.int32, (HG_CHUNK, HG_CHUNK), 1)
    tri = (col <= row).astype(F32)
    offmask = col < (row // HG_SUB) * HG_SUB
    sub_row = lax.broadcasted_iota(jnp.int32, (HG_SUB, 1), 0)
    nsub = HG_CHUNK // HG_SUB

    def chunk_body(c, carry):
        r0 = pl.multiple_of(c * HG_CHUNK, HG_CHUNK)
        rows = pl.ds(r0, HG_CHUNK)
        gc = g_scr[rows, :]
        qc = q_scr[rows, :]
        kc = k_scr[rows, :]
        vc = v_scr[rows, :]
        bc = jnp.dot(tri, gc, preferred_element_type=F32, precision=HIGHEST)
        b_scr[...] = bc
        b_last = bc[HG_CHUNK - 1:HG_CHUNK]
        q_hat = (qc * jnp.exp(bc)).astype(BF16)
        k_hat = (kc * jnp.exp(b_last - bc)).astype(BF16)
        dec = jnp.exp(b_last)
        vb = vc.astype(BF16)
        q_t, k_t = [], []
        for blk in range(1, nsub):
            lo = blk * HG_SUB
            ref = bc[lo - 1:lo]
            q_t.append((qc[lo:lo + HG_SUB] * jnp.exp(bc[lo:lo + HG_SUB] - ref)).astype(BF16))
            k_t.append((kc * jnp.exp(jnp.minimum(ref - bc, 0.0))).astype(BF16))
        outs = []
        for hd in range(HG_HEADS):
            ls = slice(hd * HG_DK, (hd + 1) * HG_DK)
            st = st_scr[hd]
            inter = lax.dot_general(q_hat[:, ls], st.astype(BF16), (((1,), (1,)), ((), ())),
                                    preferred_element_type=F32)
            sc = [jnp.zeros((HG_SUB, HG_CHUNK), F32)]
            for blk in range(1, nsub):
                sc.append(lax.dot_general(q_t[blk - 1][:, ls], k_t[blk - 1][:, ls],
                                          (((1,), (1,)), ((), ())), preferred_element_type=F32))
            scores = jnp.where(offmask, jnp.concatenate(sc, axis=0), 0.0).astype(BF16)
            intra = jnp.dot(scores, vb[:, ls], preferred_element_type=F32)
            outs.append(inter + intra)
            upd = lax.dot_general(vb[:, ls], k_hat[:, ls], (((0,), (0,)), ((), ())),
                                  preferred_element_type=F32)
            st_scr[hd] = st * dec[:, ls] + upd
        o_scr[rows, :] = jnp.concatenate(outs, axis=1)

        def diag_body(s, carry2):
            lo = pl.multiple_of((s // HG_SUB) * HG_SUB, HG_SUB)
            j = s - lo
            bs = b_scr[pl.ds(s, 1), :]
            ks = k_scr[pl.ds(r0 + s, 1), :]
            vs = v_scr[pl.ds(r0 + s, 1), :]
            qi_ = q_scr[pl.ds(r0 + lo, HG_SUB), :]
            bi_ = b_scr[pl.ds(lo, HG_SUB), :]
            a = qi_ * jnp.exp(jnp.minimum(bi_ - bs, 0.0)) * ks
            keep = sub_row >= j
            parts = []
            for hd in range(HG_HEADS):
                ls = slice(hd * HG_DK, (hd + 1) * HG_DK)
                rs = jnp.where(keep, jnp.sum(a[:, ls], axis=-1, keepdims=True), 0.0)
                parts.append(rs * vs[:, ls])
            orow = pl.ds(r0 + lo, HG_SUB)
            o_scr[orow, :] = o_scr[orow, :] + jnp.concatenate(parts, axis=1)
            return carry2

        lax.fori_loop(0, HG_CHUNK, diag_body, 0)
        return carry

    lax.fori_loop(0, tm // HG_CHUNK, chunk_body, 0)

    o = o_scr[...]
    hgn = hgn_ref[...]
    parts = []
    for hd in range(HG_HEADS):
        ls = slice(hd * HG_DK, (hd + 1) * HG_DK)
        oh = o[:, ls]
        ms = jnp.mean(oh * oh, axis=-1, keepdims=True)
        parts.append(oh * lax.rsqrt(ms + EPS) * hgn[:, ls])
    hg = proj(base + 3 * fdim, fdim)
    y_b = jnp.concatenate(parts, axis=1) * (hg * jax.nn.sigmoid(hg))
    y_ref[0, :, cdim:cdim + fdim] = y_b.astype(y_ref.dtype)


def _l0_mix(x, mods, g, w_in, conv_w, hg_norm, lb_logits, *, layer, tm):
    bsz, s, d = x.shape
    n_in = w_in.shape[1]
    cdim = conv_w.shape[1]
    fdim = HG_HEADS * HG_DK
    kern = functools.partial(_l0_kernel, tm=tm, layer=layer)
    return pl.pallas_call(
        kern,
        grid=(bsz, s // tm),
        in_specs=[
            pl.BlockSpec((1, tm, d), lambda b, i: (b, i, 0)),
            pl.BlockSpec((1, 1, 6, d), lambda b, i: (layer, b, 0, 0)),
            pl.BlockSpec((1, d), lambda b, i: (0, 0)),
            pl.BlockSpec((d, n_in), lambda b, i: (0, 0)),
            pl.BlockSpec(conv_w.shape, lambda b, i: (0, 0)),
            pl.BlockSpec((1, fdim), lambda b, i: (0, 0)),
            pl.BlockSpec(lb_logits.shape, lambda b, i: (0, 0)),
        ],
        out_specs=pl.BlockSpec((1, tm, cdim + fdim), lambda b, i: (b, i, 0)),
        out_shape=jax.ShapeDtypeStruct((bsz, s, cdim + fdim), BF16),
        scratch_shapes=[
            pltpu.VMEM((tm, d), BF16),
            pltpu.VMEM((tm + 2 * SUBLANES, cdim), F32),
            pltpu.VMEM((tm, fdim), F32),
            pltpu.VMEM((tm, fdim), F32),
            pltpu.VMEM((tm, fdim), F32),
            pltpu.VMEM((tm, fdim), F32),
            pltpu.VMEM((HG_CHUNK, fdim), F32),
            pltpu.VMEM((tm, fdim), F32),
            pltpu.VMEM((HG_HEADS, HG_DK, HG_DK), F32),
        ],
        compiler_params=_cparams(("arbitrary", "arbitrary")),
        name="l0_mix",
    )(x, mods, g.reshape(1, d), w_in, conv_w, hg_norm.reshape(1, fdim), lb_logits)


def _post_kernel(x_ref, y_ref, mod_ref, g_ref, wo_ref, w1_ref, w2_ref, o_ref, *, transposed, ff_chunk):
    mod = mod_ref[0, 0]
    if transposed:
        y = jnp.concatenate([y_ref[0, j].astype(F32).T for j in range(y_ref.shape[1])], axis=0).astype(BF16)
    else:
        y = y_ref[0]
    x1 = x_ref[0] + mod[2:3] * jnp.dot(y, wo_ref[...], preferred_element_type=F32)
    hb = _norm_mod(x1, g_ref[...], mod[4:5], mod[3:4]).astype(BF16)
    acc = jnp.zeros(x1.shape, F32)
    for c0 in range(0, w1_ref.shape[1], ff_chunk):
        a = jnp.dot(hb, w1_ref[:, c0:c0 + ff_chunk], preferred_element_type=F32)
        a = jnp.square(jnp.maximum(a, 0.0)).astype(BF16)
        acc = acc + jnp.dot(a, w2_ref[c0:c0 + ff_chunk, :], preferred_element_type=F32)
    o_ref[0] = x1 + mod[5:6] * acc


def _post(x, y, mods, g, w_out, w1, w2, *, layer, tm, transposed):
    bsz, s, d = x.shape
    dff = w1.shape[1]
    if transposed:
        blk = y.shape[3]
        y_spec = pl.BlockSpec((1, tm // blk, y.shape[2], blk), lambda b, i: (b, i, 0, 0))
    else:
        y_spec = pl.BlockSpec((1, tm, y.shape[2]), lambda b, i: (b, i, 0))
    kern = functools.partial(_post_kernel, transposed=transposed, ff_chunk=min(dff, 1024))
    return pl.pallas_call(
        kern,
        grid=(bsz, s // tm),
        in_specs=[
            pl.BlockSpec((1, tm, d), lambda b, i: (b, i, 0)),
            y_spec,
            pl.BlockSpec((1, 1, 6, d), lambda b, i: (layer, b, 0, 0)),
            pl.BlockSpec((1, d), lambda b, i: (0, 0)),
            pl.BlockSpec(w_out.shape, lambda b, i: (0, 0)),
            pl.BlockSpec(w1.shape, lambda b, i: (0, 0)),
            pl.BlockSpec(w2.shape, lambda b, i: (0, 0)),
        ],
        out_specs=pl.BlockSpec((1, tm, d), lambda b, i: (b, i, 0)),
        out_shape=jax.ShapeDtypeStruct((bsz, s, d), F32),
        compiler_params=_cparams(("arbitrary", "arbitrary")),
        name="post_t" if transposed else "post",
    )(x, y, mods, g.reshape(1, d), w_out, w1, w2)


def _qkv_kernel(x_ref, mod_ref, g_ref, wq_ref, wkv_ref, qn_ref, kn_ref, o_ref, h_scr, *, tm, q_scale):
    mod = mod_ref[0, 0]
    h = _norm_mod(x_ref[0], g_ref[...], mod[1:2], mod[0:1])
    nlane = h_scr.shape[2]
    for cc in range(h_scr.shape[0]):
        h_scr[cc] = h[:, cc * nlane:(cc + 1) * nlane]
    nt =(((1,), (1,)), ((), ()))
    width = SB_HEADS * SB_HEAD_DIM
    seg = ATT_BLOCK // SUBLANES

    def head_norm(t, gain, mult):
        parts = []
        for hd in range(SB_HEADS):
            th = t[hd * SB_HEAD_DIM:(hd + 1) * SB_HEAD_DIM]
            ms = jnp.mean(th * th, axis=0, keepdims=True)
            parts.append(th * lax.rsqrt(ms + EPS) * gain * mult)
        return jnp.concatenate(parts, axis=0)

    q_t = lax.dot_general(wq_ref[...], h.astype(BF16), nt, preferred_element_type=F32)
    q_t = head_norm(q_t, qn_ref[...], q_scale).astype(o_ref.dtype)
    cols = []
    for cc in range(h_scr.shape[0]):
        pieces = []
        for blk in range(tm // ATT_BLOCK):
            for j in range(seg):
                pieces.append(h_scr[cc, pl.ds(blk * ATT_BLOCK + j, SUBLANES, stride=seg), :])
        cols.append(jnp.concatenate(pieces, axis=0))
    hp = jnp.concatenate(cols, axis=1).astype(BF16)
    kv_t = lax.dot_general(wkv_ref[...], hp, nt, preferred_element_type=F32)
    k_t = head_norm(kv_t[0:width], kn_ref[...], 1.0).astype(o_ref.dtype)
    v_t = kv_t[width:2 * width].astype(o_ref.dtype)
    for blk in range(tm // ATT_BLOCK):
        ls = slice(blk * ATT_BLOCK, (blk + 1) * ATT_BLOCK)
        o_ref[0, blk, 0:width, :] = q_t[:, ls]
        o_ref[0, blk, width:2 * width, :] = k_t[:, ls]
        o_ref[0, blk, 2 * width:3 * width, :] = v_t[:, ls]


def _qkv(x, mods, g, wq_t, wkv_t, q_norm, k_norm, *, layer, tm):
    bsz, s, d = x.shape
    width = SB_HEADS * SB_HEAD_DIM
    q_scale = SB_HEAD_DIM ** -0.5 * math.log2(math.e)
    kern = functools.partial(_qkv_kernel, tm=tm, q_scale=q_scale)
    return pl.pallas_call(
        kern,
        grid=(bsz, s // tm),
        in_specs=[
            pl.BlockSpec((1, tm, d), lambda b, i: (b, i, 0)),
            pl.BlockSpec((1, 1, 6, d), lambda b, i: (layer, b, 0, 0)),
            pl.BlockSpec((1, d), lambda b, i: (0, 0)),
            pl.BlockSpec(wq_t.shape, lambda b, i: (0, 0)),
            pl.BlockSpec(wkv_t.shape, lambda b, i: (0, 0)),
            pl.BlockSpec((SB_HEAD_DIM, 1), lambda b, i: (0, 0)),
            pl.BlockSpec((SB_HEAD_DIM, 1), lambda b, i: (0, 0)),
        ],
        out_specs=pl.BlockSpec((1, tm // ATT_BLOCK, 3 * width, ATT_BLOCK), lambda b, i: (b, i, 0, 0)),
        out_shape=jax.ShapeDtypeStruct((bsz, s // ATT_BLOCK, 3 * width, ATT_BLOCK), BF16),
        scratch_shapes=[pltpu.VMEM((d // LANES, tm, LANES), F32)],
        compiler_params=_cparams(("arbitrary", "arbitrary")),
        name="qkv",
    )(x, mods, g.reshape(1, d), wq_t, wkv_t, q_norm.reshape(SB_HEAD_DIM, 1), k_norm.reshape(SB_HEAD_DIM, 1))


def _attn_kernel(q_ref, k_ref, v_ref, o_ref, z_scr, w1_scr, w_scr, acc_scr):
    qi = pl.program_id(2)
    blk = ATT_BLOCK
    seg = blk // SUBLANES
    q_t = q_ref[0, 0]
    acc_scr[...] = jnp.zeros_like(acc_scr)
    sub = lax.broadcasted_iota(jnp.int32, (SUBLANES, blk), 0)
    lane = lax.broadcasted_iota(jnp.int32, (SUBLANES, blk), 1)
    ones = jnp.ones((SUBLANES, blk), F32)

    def process(kblk, carry, masked):
        z_scr[...] = lax.dot_general(k_ref[0, kblk], q_t, (((0,), (0,)), ((), ())),
                                     preferred_element_type=F32)
        c = ones
        for j in reversed(range(seg)):
            rows = slice(j * SUBLANES, (j + 1) * SUBLANES)
            z = z_scr[rows, :]
            t = jnp.exp2(-jnp.abs(z))
            r = 1.0 / (1.0 + t)
            tr = t * r
            pos = z >= 0.0
            beta = jnp.where(pos, r, tr)
            omb = jnp.where(pos, tr, r)
            if masked:
                valid = (sub * seg + j) < lane
                beta = jnp.where(valid, beta, 0.0)
                omb = jnp.where(valid, omb, 1.0)
            w1_scr[rows, :] = beta * c
            c = c * omb
        inc = c
        for d in (1, 2, 4):
            sh = pltpu.roll(inc, SUBLANES - d, axis=0)
            inc = inc * jnp.where(sub + d < SUBLANES, sh, 1.0)
        excl = jnp.where(sub + 1 < SUBLANES, pltpu.roll(inc, SUBLANES - 1, axis=0), 1.0)
        scale = excl * carry
        scale2 = jnp.concatenate([scale, scale], axis=0)
        for j in range(seg // 2):
            rows = slice(j * 2 * SUBLANES, (j + 1) * 2 * SUBLANES)
            w_scr[rows, :] = (w1_scr[rows, :] * scale2).astype(w_scr.dtype)
        acc_scr[...] += jnp.dot(v_ref[0, kblk], w_scr[...], preferred_element_type=F32)
        return carry * jnp.broadcast_to(inc[0:1], (SUBLANES, blk))

    carry = process(qi, ones, True)
    lax.fori_loop(0, qi, lambda n, cr: process(qi - 1 - n, cr, False), carry)
    o_ref[0, 0] = acc_scr[...].astype(o_ref.dtype)


def _attn(qkv_t):
    bsz, ns, rows, blk = qkv_t.shape
    width = SB_HEADS * SB_HEAD_DIM
    return pl.pallas_call(
        _attn_kernel,
        grid=(bsz, SB_HEADS, ns),
        in_specs=[
            pl.BlockSpec((1, 1, SB_HEAD_DIM, blk), lambda b, h, i: (b, i, h, 0)),
            pl.BlockSpec((1, ns, SB_HEAD_DIM, blk), lambda b, h, i: (b, 0, SB_HEADS + h, 0)),
            pl.BlockSpec((1, ns, SB_HEAD_DIM, blk), lambda b, h, i: (b, 0, 2 * SB_HEADS + h, 0)),
        ],
        out_specs=pl.BlockSpec((1, 1, SB_HEAD_DIM, blk), lambda b, h, i: (b, i, h, 0)),
        out_shape=jax.ShapeDtypeStruct((bsz, ns, width, blk), BF16),
        scratch_shapes=[
            pltpu.VMEM((blk, blk), F32),
            pltpu.VMEM((blk, blk), F32),
            pltpu.VMEM((blk, blk), BF16),
            pltpu.VMEM((SB_HEAD_DIM, blk), F32),
        ],
        compiler_params=_cparams(("arbitrary", "arbitrary", "arbitrary")),
        name="attn",
    )(qkv_t, qkv_t, qkv_t)


def kernel(x, c, ada_w, ada_b, norm_mix, norm_mlp, w_in_ab, conv_w, hg_norm, lb_logits, w_out_ab,
           w_qkv, q_norm, k_norm, w_out_c, mlp_w1, mlp_w2):
    bsz, s, d = x.shape
    depth = ada_w.shape[0]
    tm = min(s, 512)
    width = SB_HEADS * SB_HEAD_DIM
    mods = _ada(c, ada_w, ada_b).reshape(depth, bsz, 6, d)
    for layer in range(depth):
        j = layer // 2
        w1 = mlp_w1[layer].astype(BF16)
        w2 = mlp_w2[layer].astype(BF16)
        if layer % 2 == 0:
            y = _l0_mix(x, mods, norm_mix[layer], w_in_ab[j].astype(BF16), conv_w[j], hg_norm[j],
                        lb_logits, layer=layer, tm=min(s, 256))
            x = _post(x, y, mods, norm_mlp[layer], w_out_ab[j].astype(BF16), w1, w2,
                      layer=layer, tm=tm, transposed=False)
        else:
            wq_t = w_qkv[j][:, 0:width].T.astype(BF16)
            wkv_t = w_qkv[j][:, width:3 * width].T.astype(BF16)
            qkv_t = _qkv(x, mods, norm_mix[layer], wq_t, wkv_t, q_norm[j], k_norm[j], layer=layer, tm=tm)
            o_t = _attn(qkv_t)
            x = _post(x, o_t, mods, norm_mlp[layer], w_out_c[j].astype(BF16), w1, w2,
                      layer=layer, tm=tm, transposed=True)
    return x
```

```python
import functools

import jax
import jax.numpy as jnp
from jax import lax
from jax.experimental import pallas as pl
from jax.experimental.pallas import tpu as pltpu

F32 = jnp.float32
BF16 = jnp.bfloat16
HIGHEST = lax.Precision.HIGHEST

EPS = 1e-6
HG_CHUNK = 64
HG_SUB = 16
HG_FAST_MAX_EXP = 60.0
HG_HEADS = 4
HG_DK = 128
SB_HEADS = 16
SB_HEAD_DIM = 64
ATT_BLOCK = 256
ATT_HEADS_PER_STEP = 16
ATT_QK_AHEAD = 4
SUBLANES = 8
LANES = 128
VMEM_LIMIT = 56 * 1024 * 1024


def _cparams(sem):
    return pltpu.CompilerParams(dimension_semantics=sem, vmem_limit_bytes=VMEM_LIMIT)


def _norm_mod(x, g, scale, shift):
    ms = jnp.mean(x * x, axis=-1, keepdims=True)
    return (x * lax.rsqrt(ms + EPS) * g) * (1.0 + scale) + shift


def _ada_kernel(c_ref, w_ref, b_ref, o_ref):
    c = c_ref[...]
    ca = c * jax.nn.sigmoid(c)
    o_ref[0] = jnp.dot(ca, w_ref[0], preferred_element_type=F32, precision=HIGHEST) + b_ref[0]


def _ada(c, ada_w, ada_b):
    depth, d, n = ada_w.shape
    bsz = c.shape[0]
    tn = 1536 if n % 1536 == 0 else n
    return pl.pallas_call(
        _ada_kernel,
        grid=(depth, n // tn),
        in_specs=[
            pl.BlockSpec((bsz, d), lambda l, j: (0, 0)),
            pl.BlockSpec((1, d, tn), lambda l, j: (l, 0, j)),
            pl.BlockSpec((1, 1, tn), lambda l, j: (l, 0, j)),
        ],
        out_specs=pl.BlockSpec((1, bsz, tn), lambda l, j: (l, 0, j)),
        out_shape=jax.ShapeDtypeStruct((depth, bsz, n), F32),
        compiler_params=_cparams(("arbitrary", "arbitrary")),
        name="ada",
    )(c, ada_w, ada_b.reshape(depth, 1, n))


def _l0_kernel(x_ref, mod_ref, g_ref, w_ref, cw_ref, hgn_ref, lb_ref, y_ref,
               hb_scr, p_scr, q_scr, k_scr, v_scr, g_scr, b_scr, o_scr, st_scr,
               *, tm, layer):
    i = pl.program_id(1)
    cdim = cw_ref.shape[1]
    fdim = HG_HEADS * HG_DK

    @pl.when(i == 0)
    def _():
        st_scr[...] = jnp.zeros_like(st_scr)
        p_scr[0:SUBLANES, :] = jnp.zeros((SUBLANES, cdim), F32)

    mod = mod_ref[0, 0]
    h = _norm_mod(x_ref[0], g_ref[...], mod[1:2], mod[0:1])
    hb_scr[...] = h.astype(BF16)

    def proj(j, width):
        return jnp.dot(hb_scr[...], w_ref[:, j:j + width], preferred_element_type=F32)

    a_b = proj(0, cdim)
    p = proj(cdim, cdim) * proj(2 * cdim, cdim)
    p_scr[SUBLANES:SUBLANES + tm, :] = p
    p1 = p_scr[SUBLANES - 1:SUBLANES - 1 + tm, :]
    p2 = p_scr[SUBLANES - 2:SUBLANES - 2 + tm, :]
    cw = cw_ref[...]
    y_a = a_b * (cw[0:1] * p2 + cw[1:2] * p1 + cw[2:3] * p)
    p_scr[0:SUBLANES, :] = p_scr[tm:tm + SUBLANES, :]
    y_ref[0, :, 0:cdim] = y_a.astype(y_ref.dtype)

    base = 3 * cdim
    lbl = lb_ref[...]
    e = jnp.exp(lbl - jnp.max(lbl, axis=0, keepdims=True))
    lb = jnp.sum(e[0:layer + 1], axis=0, keepdims=True) / jnp.sum(e, axis=0, keepdims=True)
    q_scr[...] = proj(base, fdim)
    f = lb + (1.0 - lb) * jax.nn.sigmoid(proj(base + fdim, fdim))
    g_scr[...] = jnp.log(f)
    k_scr[...] = 1.0 - f
    v_scr[...] = proj(base + 2 * fdim, fdim)

    nchunk = tm // HG_CHUNK
    nsub = HG_CHUNK // HG_SUB
    nt = (((1,), (1,)), ((), ()))
    tn = (((0,), (0,)), ((), ()))
    row = lax.broadcasted_iota(jnp.int32, (tm, tm), 0)
    col = lax.broadcasted_iota(jnp.int32, (tm, tm), 1)
    causal = (col <= row) & (col // HG_CHUNK == row // HG_CHUNK)
    b_scr[...] = jnp.dot(causal.astype(F32), g_scr[...], preferred_element_type=F32, precision=HIGHEST)
    ends = jnp.concatenate([b_scr[(c + 1) * HG_CHUNK - 1:(c + 1) * HG_CHUNK, :] for c in range(nchunk)], axis=0)
    max_decay = jnp.max(-ends)

    def fast_tile():
        bt = b_scr[...]
        kt = k_scr[...]
        q_hat = (q_scr[...] * jnp.exp(bt)).astype(BF16)
        k_bar = (kt * jnp.exp(-bt)).astype(BF16)
        vb = v_scr[...].astype(BF16)
        outs = []
        for hd in range(HG_HEADS):
            ls = slice(hd * HG_DK, (hd + 1) * HG_DK)
            sc = lax.dot_general(q_hat[:, ls], k_bar[:, ls], nt, preferred_element_type=F32)
            sc = jnp.where(causal, sc, 0.0).astype(BF16)
            intra = jnp.dot(sc, vb[:, ls], preferred_element_type=F32)
            st = st_scr[hd]
            inter = []
            for c in range(nchunk):
                rs = slice(c * HG_CHUNK, (c + 1) * HG_CHUNK)
                inter.append(lax.dot_general(q_hat[rs, ls], st.astype(BF16), nt, preferred_element_type=F32))
                b_last = bt[(c + 1) * HG_CHUNK - 1:(c + 1) * HG_CHUNK, ls]
                k_hat = (kt[rs, ls] * jnp.exp(b_last - bt[rs, ls])).astype(BF16)
                st = st * jnp.exp(b_last) + lax.dot_general(vb[rs, ls], k_hat, tn, preferred_element_type=F32)
            st_scr[hd] = st
            outs.append(intra + jnp.concatenate(inter, axis=0))
        o_scr[...] = jnp.concatenate(outs, axis=1)

    def safe_tile():
        srow = lax.broadcasted_iota(jnp.int32, (HG_CHUNK, HG_CHUNK), 0)
        scol = lax.broadcasted_iota(jnp.int32, (HG_CHUNK, HG_CHUNK), 1)
        offmask = scol < (srow // HG_SUB) * HG_SUB
        sub_row = lax.broadcasted_iota(jnp.int32, (HG_SUB, 1), 0)

        def chunk_body(c, carry):
            r0 = pl.multiple_of(c * HG_CHUNK, HG_CHUNK)
            rows = pl.ds(r0, HG_CHUNK)
            qc = q_scr[rows, :]
            kc = k_scr[rows, :]
            bc = b_scr[rows, :]
            b_last = bc[HG_CHUNK - 1:HG_CHUNK]
            q_hat = (qc * jnp.exp(bc)).astype(BF16)
            k_hat = (kc * jnp.exp(b_last - bc)).astype(BF16)
            dec = jnp.exp(b_last)
            vb = v_scr[rows, :].astype(BF16)
            q_t, k_t = [], []
            for blk in range(1, nsub):
                lo = blk * HG_SUB
                ref = bc[lo - 1:lo]
                q_t.append((qc[lo:lo + HG_SUB] * jnp.exp(bc[lo:lo + HG_SUB] - ref)).astype(BF16))
                k_t.append((kc * jnp.exp(jnp.minimum(ref - bc, 0.0))).astype(BF16))
            outs = []
            for hd in range(HG_HEADS):
                ls = slice(hd * HG_DK, (hd + 1) * HG_DK)
                st = st_scr[hd]
                inter = lax.dot_general(q_hat[:, ls], st.astype(BF16), nt, preferred_element_type=F32)
                sc = [jnp.zeros((HG_SUB, HG_CHUNK), F32)]
                for blk in range(1, nsub):
                    sc.append(lax.dot_general(q_t[blk - 1][:, ls], k_t[blk - 1][:, ls], nt,
                                              preferred_element_type=F32))
                scores = jnp.where(offmask, jnp.concatenate(sc, axis=0), 0.0).astype(BF16)
                outs.append(inter + jnp.dot(scores, vb[:, ls], preferred_element_type=F32))
                upd = lax.dot_general(vb[:, ls], k_hat[:, ls], tn, preferred_element_type=F32)
                st_scr[hd] = st * dec[:, ls] + upd
            o_scr[rows, :] = jnp.concatenate(outs, axis=1)

            def diag_body(s, carry2):
                lo = pl.multiple_of((s // HG_SUB) * HG_SUB, HG_SUB)
                bs = b_scr[pl.ds(r0 + s, 1), :]
                ks = k_scr[pl.ds(r0 + s, 1), :]
                vs = v_scr[pl.ds(r0 + s, 1), :]
                qi_ = q_scr[pl.ds(r0 + lo, HG_SUB), :]
                bi_ = b_scr[pl.ds(r0 + lo, HG_SUB), :]
                a = qi_ * jnp.exp(jnp.minimum(bi_ - bs, 0.0)) * ks
                keep = sub_row >= s - lo
                parts = []
                for hd in range(HG_HEADS):
                    ls = slice(hd * HG_DK, (hd + 1) * HG_DK)
                    rsum = jnp.where(keep, jnp.sum(a[:, ls], axis=-1, keepdims=True), 0.0)
                    parts.append(rsum * vs[:, ls])
                orow = pl.ds(r0 + lo, HG_SUB)
                o_scr[orow, :] = o_scr[orow, :] + jnp.concatenate(parts, axis=1)
                return carry2

            lax.fori_loop(0, HG_CHUNK, diag_body, 0)
            return carry

        lax.fori_loop(0, nchunk, chunk_body, 0)

    lax.cond(max_decay <= HG_FAST_MAX_EXP, fast_tile, safe_tile)

    o = o_scr[...]
    hgn = hgn_ref[...]
    parts = []
    for hd in range(HG_HEADS):
        ls = slice(hd * HG_DK, (hd + 1) * HG_DK)
        oh = o[:, ls]
        ms = jnp.mean(oh * oh, axis=-1, keepdims=True)
        parts.append(oh * lax.rsqrt(ms + EPS) * hgn[:, ls])
    hg = proj(base + 3 * fdim, fdim)
    y_b = jnp.concatenate(parts, axis=1) * (hg * jax.nn.sigmoid(hg))
    y_ref[0, :, cdim:cdim + fdim] = y_b.astype(y_ref.dtype)


def _l0_mix(x, mods, g, w_in, conv_w, hg_norm, lb_logits, *, layer, tm):
    bsz, s, d = x.shape
    n_in = w_in.shape[1]
    cdim = conv_w.shape[1]
    fdim = HG_HEADS * HG_DK
    kern = functools.partial(_l0_kernel, tm=tm, layer=layer)
    return pl.pallas_call(
        kern,
        grid=(bsz, s // tm),
        in_specs=[
            pl.BlockSpec((1, tm, d), lambda b, i: (b, i, 0)),
            pl.BlockSpec((1, 1, 6, d), lambda b, i: (layer, b, 0, 0)),
            pl.BlockSpec((1, d), lambda b, i: (0, 0)),
            pl.BlockSpec((d, n_in), lambda b, i: (0, 0)),
            pl.BlockSpec(conv_w.shape, lambda b, i: (0, 0)),
            pl.BlockSpec((1, fdim), lambda b, i: (0, 0)),
            pl.BlockSpec(lb_logits.shape, lambda b, i: (0, 0)),
        ],
        out_specs=pl.BlockSpec((1, tm, cdim + fdim), lambda b, i: (b, i, 0)),
        out_shape=jax.ShapeDtypeStruct((bsz, s, cdim + fdim), BF16),
        scratch_shapes=[
            pltpu.VMEM((tm, d), BF16),
            pltpu.VMEM((tm + 2 * SUBLANES, cdim), F32),
            pltpu.VMEM((tm, fdim), F32),
            pltpu.VMEM((tm, fdim), F32),
            pltpu.VMEM((tm, fdim), F32),
            pltpu.VMEM((tm, fdim), F32),
            pltpu.VMEM((tm, fdim), F32),
            pltpu.VMEM((tm, fdim), F32),
            pltpu.VMEM((HG_HEADS, HG_DK, HG_DK), F32),
        ],
        compiler_params=_cparams(("arbitrary", "arbitrary")),
        name="l0_mix",
    )(x, mods, g.reshape(1, d), w_in, conv_w, hg_norm.reshape(1, fdim), lb_logits)


def _post_kernel(x_ref, y_ref, mod_ref, g_ref, wo_ref, w1_ref, w2_ref, o_ref, *, transposed, ff_chunk):
    mod = mod_ref[0, 0]
    if transposed:
        y = jnp.concatenate([y_ref[0, j].astype(F32).T for j in range(y_ref.shape[1])], axis=0).astype(BF16)
    else:
        y = y_ref[0]
    x1 = x_ref[0] + mod[2:3] * jnp.dot(y, wo_ref[...], preferred_element_type=F32)
    hb = _norm_mod(x1, g_ref[...], mod[4:5], mod[3:4]).astype(BF16)
    acc = jnp.zeros(x1.shape, F32)
    for c0 in range(0, w1_ref.shape[1], ff_chunk):
        a = jnp.dot(hb, w1_ref[:, c0:c0 + ff_chunk], preferred_element_type=F32)
        a = jnp.square(jnp.maximum(a, 0.0)).astype(BF16)
        acc = acc + jnp.dot(a, w2_ref[c0:c0 + ff_chunk, :], preferred_element_type=F32)
    o_ref[0] = x1 + mod[5:6] * acc


def _post(x, y, mods, g, w_out, w1, w2, *, layer, tm, transposed):
    bsz, s, d = x.shape
    dff = w1.shape[1]
    if transposed:
        blk = y.shape[3]
        y_spec = pl.BlockSpec((1, tm // blk, y.shape[2], blk), lambda b, i: (b, i, 0, 0))
    else:
        y_spec = pl.BlockSpec((1, tm, y.shape[2]), lambda b, i: (b, i, 0))
    kern = functools.partial(_post_kernel, transposed=transposed, ff_chunk=min(dff, 1024))
    return pl.pallas_call(
        kern,
        grid=(bsz, s // tm),
        in_specs=[
            pl.BlockSpec((1, tm, d), lambda b, i: (b, i, 0)),
            y_spec,
            pl.BlockSpec((1, 1, 6, d), lambda b, i: (layer, b, 0, 0)),
            pl.BlockSpec((1, d), lambda b, i: (0, 0)),
            pl.BlockSpec(w_out.shape, lambda b, i: (0, 0)),
            pl.BlockSpec(w1.shape, lambda b, i: (0, 0)),
            pl.BlockSpec(w2.shape, lambda b, i: (0, 0)),
        ],
        out_specs=pl.BlockSpec((1, tm, d), lambda b, i: (b, i, 0)),
        out_shape=jax.ShapeDtypeStruct((bsz, s, d), F32),
        compiler_params=_cparams(("arbitrary", "arbitrary")),
        name="post_t" if transposed else "post",
    )(x, y, mods, g.reshape(1, d), w_out, w1, w2)


def _qkv_kernel(x_ref, mod_ref, g_ref, wq_ref, wkv_ref, qn_ref, kn_ref, o_ref, h_scr, *, tm, q_scale):
    mod = mod_ref[0, 0]
    h = _norm_mod(x_ref[0], g_ref[...], mod[1:2], mod[0:1])
    nlane = h_scr.shape[2]
    for cc in range(h_scr.shape[0]):
        h_scr[cc] = h[:, cc * nlane:(cc + 1) * nlane]
    nt = (((1,), (1,)), ((), ()))
    width = SB_HEADS * SB_HEAD_DIM
    seg = ATT_BLOCK // SUBLANES

    def head_norm(t, gain, mult):
        parts = []
        for hd in range(SB_HEADS):
            th = t[hd * SB_HEAD_DIM:(hd + 1) * SB_HEAD_DIM]
            ms = jnp.mean(th * th, axis=0, keepdims=True)
            parts.append(th * lax.rsqrt(ms + EPS) * gain * mult)
        return jnp.concatenate(parts, axis=0)

    q_t = lax.dot_general(wq_ref[...], h.astype(BF16), nt, preferred_element_type=F32)
    q_t = head_norm(q_t, qn_ref[...], q_scale).astype(o_ref.dtype)
    cols = []
    for cc in range(h_scr.shape[0]):
        pieces = []
        for blk in range(tm // ATT_BLOCK):
            for j in range(seg):
                pieces.append(h_scr[cc, pl.ds(blk * ATT_BLOCK + j, SUBLANES, stride=seg), :])
        cols.append(jnp.concatenate(pieces, axis=0))
    hp = jnp.concatenate(cols, axis=1).astype(BF16)
    kv_t = lax.dot_general(wkv_ref[...], hp, nt, preferred_element_type=F32)
    k_t = head_norm(kv_t[0:width], kn_ref[...], 1.0).astype(o_ref.dtype)
    v_t = kv_t[width:2 * width].astype(o_ref.dtype)
    for blk in range(tm // ATT_BLOCK):
        ls = slice(blk * ATT_BLOCK, (blk + 1) * ATT_BLOCK)
        o_ref[0, blk, 0:width, :] = q_t[:, ls]
        o_ref[0, blk, width:2 * width, :] = k_t[:, ls]
        o_ref[0, blk, 2 * width:3 * width, :] = v_t[:, ls]


def _qkv(x, mods, g, wq_t, wkv_t, q_norm, k_norm, *, layer, tm):
    bsz, s, d = x.shape
    width = SB_HEADS * SB_HEAD_DIM
    q_scale = 0.5 * SB_HEAD_DIM ** -0.5
    kern = functools.partial(_qkv_kernel, tm=tm, q_scale=q_scale)
    return pl.pallas_call(
        kern,
        grid=(bsz, s // tm),
        in_specs=[
            pl.BlockSpec((1, tm, d), lambda b, i: (b, i, 0)),
            pl.BlockSpec((1, 1, 6, d), lambda b, i: (layer, b, 0, 0)),
            pl.BlockSpec((1, d), lambda b, i: (0, 0)),
            pl.BlockSpec(wq_t.shape, lambda b, i: (0, 0)),
            pl.BlockSpec(wkv_t.shape, lambda b, i: (0, 0)),
            pl.BlockSpec((SB_HEAD_DIM, 1), lambda b, i: (0, 0)),
            pl.BlockSpec((SB_HEAD_DIM, 1), lambda b, i: (0, 0)),
        ],
        out_specs=pl.BlockSpec((1, tm // ATT_BLOCK, 3 * width, ATT_BLOCK), lambda b, i: (b, i, 0, 0)),
        out_shape=jax.ShapeDtypeStruct((bsz, s // ATT_BLOCK, 3 * width, ATT_BLOCK), BF16),
        scratch_shapes=[pltpu.VMEM((d // LANES, tm, LANES), F32)],
        compiler_params=_cparams(("arbitrary", "arbitrary")),
        name="qkv",
    )(x, mods, g.reshape(1, d), wq_t, wkv_t, q_norm.reshape(SB_HEAD_DIM, 1), k_norm.reshape(SB_HEAD_DIM, 1))


def _attn_kernel(q_ref, k_ref, v_ref, o_ref, w1_scr, w_scr, acc_scr, car_scr, *, heads):
    qi = pl.program_id(2)
    blk = ATT_BLOCK
    seg = blk // SUBLANES
    dh = SB_HEAD_DIM
    tn = (((0,), (0,)), ((), ()))
    acc_scr[...] = jnp.zeros_like(acc_scr)
    car_scr[...] = jnp.ones_like(car_scr)
    sub = lax.broadcasted_iota(jnp.int32, (SUBLANES, blk), 0)
    lane = lax.broadcasted_iota(jnp.int32, (SUBLANES, blk), 1)
    ones = jnp.ones((SUBLANES, blk), F32)

    def process(kblk, masked):
        def scores(g):
            hs = slice(g * dh, (g + 1) * dh)
            return lax.dot_general(k_ref[0, kblk, hs, :], q_ref[0, 0, hs, :], tn, preferred_element_type=F32)

        zs = {g: scores(g) for g in range(min(ATT_QK_AHEAD, heads))}
        for g in range(heads):
            hs = slice(g * dh, (g + 1) * dh)
            if g + ATT_QK_AHEAD < heads:
                zs[g + ATT_QK_AHEAD] = scores(g + ATT_QK_AHEAD)
            z = zs.pop(g)
            c = ones
            for j in reversed(range(seg)):
                rows = slice(j * SUBLANES, (j + 1) * SUBLANES)
                beta = 0.5 * jnp.tanh(z[rows, :]) + 0.5
                if masked:
                    beta = jnp.where((sub * seg + j) < lane, beta, 0.0)
                w = beta * c
                w1_scr[g, rows, :] = w
                c = c - w
            inc = c
            for d in (1, 2, 4):
                sh = pltpu.roll(inc, SUBLANES - d, axis=0)
                inc = inc * jnp.where(sub + d < SUBLANES, sh, 1.0)
            excl = jnp.where(sub + 1 < SUBLANES, pltpu.roll(inc, SUBLANES - 1, axis=0), 1.0)
            carry = car_scr[g]
            scale = excl * carry
            car_scr[g] = carry * jnp.broadcast_to(inc[0:1], (SUBLANES, blk))
            scale2 = jnp.concatenate([scale, scale], axis=0)
            for j in range(seg // 2):
                rows = slice(j * 2 * SUBLANES, (j + 1) * 2 * SUBLANES)
                w_scr[g, rows, :] = (w1_scr[g, rows, :] * scale2).astype(w_scr.dtype)
            acc_scr[hs, :] += jnp.dot(v_ref[0, kblk, hs, :], w_scr[g], preferred_element_type=F32)

    process(qi, True)

    def body(n, carry):
        process(qi - 1 - n, False)
        return carry

    lax.fori_loop(0, qi, body, 0)
    o_ref[0, 0] = acc_scr[...].astype(o_ref.dtype)


def _attn(qkv_t, *, heads):
    bsz, ns, rows, blk = qkv_t.shape
    width = SB_HEADS * SB_HEAD_DIM
    ngroup = SB_HEADS // heads
    hrows = heads * SB_HEAD_DIM
    return pl.pallas_call(
        functools.partial(_attn_kernel, heads=heads),
        grid=(bsz, ngroup, ns),
        in_specs=[
            pl.BlockSpec((1, 1, hrows, blk), lambda b, h, i: (b, i, h, 0)),
            pl.BlockSpec((1, ns, hrows, blk), lambda b, h, i: (b, 0, ngroup + h, 0)),
            pl.BlockSpec((1, ns, hrows, blk), lambda b, h, i: (b, 0, 2 * ngroup + h, 0)),
        ],
        out_specs=pl.BlockSpec((1, 1, hrows, blk), lambda b, h, i: (b, i, h, 0)),
        out_shape=jax.ShapeDtypeStruct((bsz, ns, width, blk), BF16),
        scratch_shapes=[
            pltpu.VMEM((heads, blk, blk), F32),
            pltpu.VMEM((heads, blk, blk), BF16),
            pltpu.VMEM((hrows, blk), F32),
            pltpu.VMEM((heads, SUBLANES, blk), F32),
        ],
        compiler_params=_cparams(("arbitrary", "arbitrary", "arbitrary")),
        name="attn",
    )(qkv_t, qkv_t, qkv_t)


def kernel(x, c, ada_w, ada_b, norm_mix, norm_mlp, w_in_ab, conv_w, hg_norm, lb_logits, w_out_ab,
           w_qkv, q_norm, k_norm, w_out_c, mlp_w1, mlp_w2):
    bsz, s, d = x.shape
    depth = ada_w.shape[0]
    tm = min(s, 512)
    width = SB_HEADS * SB_HEAD_DIM
    mods = _ada(c, ada_w, ada_b).reshape(depth, bsz, 6, d)
    for layer in range(depth):
        j = layer // 2
        w1 = mlp_w1[layer].astype(BF16)
        w2 = mlp_w2[layer].astype(BF16)
        if layer % 2 == 0:
            y = _l0_mix(x, mods, norm_mix[layer], w_in_ab[j].astype(BF16), conv_w[j], hg_norm[j],
                        lb_logits, layer=layer, tm=min(s, 256))
            x = _post(x, y, mods, norm_mlp[layer], w_out_ab[j].astype(BF16), w1, w2,
                      layer=layer, tm=tm, transposed=False)
        else:
            wq_t = w_qkv[j][:, 0:width].T.astype(BF16)
            wkv_t = w_qkv[j][:, width:3 * width].T.astype(BF16)
            qkv_t = _qkv(x, mods, norm_mix[layer], wq_t, wkv_t, q_norm[j], k_norm[j], layer=layer, tm=tm)
            o_t = _attn(qkv_t, heads=ATT_HEADS_PER_STEP)
            x = _post(x, o_t, mods, norm_mlp[layer], w_out_c[j].astype(BF16), w1, w2,
                      layer=layer, tm=tm, transposed=True)
    return x
```

```python
import functools

import jax
import jax.numpy as jnp
from jax import lax
from jax.experimental import pallas as pl
from jax.experimental.pallas import tpu as pltpu

F32 = jnp.float32
BF16 = jnp.bfloat16
HIGHEST = lax.Precision.HIGHEST

EPS = 1e-6
HG_CHUNK = 64
HG_SUB = 16
HG_FAST_MAX_EXP = 60.0
HG_HEADS = 4
HG_DK = 128
SB_HEADS = 16
SB_HEAD_DIM = 64
ATT_BLOCK = 256
ATT_HEADS_PER_STEP = 16
ATT_QK_AHEAD = 4
SUBLANES = 8
LANES = 128
VMEM_LIMIT = 56 * 1024 * 1024


def _cparams(sem):
    return pltpu.CompilerParams(dimension_semantics=sem, vmem_limit_bytes=VMEM_LIMIT)


def _norm_mod(x, g, scale, shift):
    ms = jnp.mean(x * x, axis=-1, keepdims=True)
    return (x * lax.rsqrt(ms + EPS) * g) * (1.0 + scale) + shift


def _ada_kernel(c_ref, w_ref, b_ref, o_ref):
    c = c_ref[...]
    ca = c * jax.nn.sigmoid(c)
    o_ref[0] = jnp.dot(ca, w_ref[0], preferred_element_type=F32, precision=HIGHEST) + b_ref[0]


def _ada(c, ada_w, ada_b):
    depth, d, n = ada_w.shape
    bsz = c.shape[0]
    tn = 1536 if n % 1536 == 0 else n
    return pl.pallas_call(
        _ada_kernel,
        grid=(depth, n // tn),
        in_specs=[
            pl.BlockSpec((bsz, d), lambda l, j: (0, 0)),
            pl.BlockSpec((1, d, tn), lambda l, j: (l, 0, j)),
            pl.BlockSpec((1, 1, tn), lambda l, j: (l, 0, j)),
        ],
        out_specs=pl.BlockSpec((1, bsz, tn), lambda l, j: (l, 0, j)),
        out_shape=jax.ShapeDtypeStruct((depth, bsz, n), F32),
        compiler_params=_cparams(("arbitrary", "arbitrary")),
        name="ada",
    )(c, ada_w, ada_b.reshape(depth, 1, n))


def _l0_kernel(x_ref, mod_ref, g_ref, w_ref, cw_ref, hgn_ref, lb_ref, y_ref,
               hb_scr, p_scr, q_scr, k_scr, v_scr, g_scr, b_scr, o_scr, st_scr,
               *, tm, layer):
    i = pl.program_id(1)
    cdim = cw_ref.shape[1]
    fdim = HG_HEADS * HG_DK

    @pl.when(i == 0)
    def _():
        st_scr[...] = jnp.zeros_like(st_scr)
        p_scr[0:SUBLANES, :] = jnp.zeros((SUBLANES, cdim), F32)

    mod = mod_ref[0, 0]
    h = _norm_mod(x_ref[0], g_ref[...], mod[1:2], mod[0:1])
    hb_scr[...] = h.astype(BF16)

    def proj(j, width):
        return jnp.dot(hb_scr[...], w_ref[:, j:j + width], preferred_element_type=F32)

    a_b = proj(0, cdim)
    p = proj(cdim, cdim) * proj(2 * cdim, cdim)
    p_scr[SUBLANES:SUBLANES + tm, :] = p
    p1 = p_scr[SUBLANES - 1:SUBLANES - 1 + tm, :]
    p2 = p_scr[SUBLANES - 2:SUBLANES - 2 + tm, :]
    cw = cw_ref[...]
    y_a = a_b * (cw[0:1] * p2 + cw[1:2] * p1 + cw[2:3] * p)
    p_scr[0:SUBLANES, :] = p_scr[tm:tm + SUBLANES, :]
    y_ref[0, :, 0:cdim] = y_a.astype(y_ref.dtype)

    base = 3 * cdim
    lbl = lb_ref[...]
    e = jnp.exp(lbl - jnp.max(lbl, axis=0, keepdims=True))
    lb = jnp.sum(e[0:layer + 1], axis=0, keepdims=True) / jnp.sum(e, axis=0, keepdims=True)
    q_scr[...] = proj(base, fdim)
    f = lb + (1.0 - lb) * jax.nn.sigmoid(proj(base + fdim, fdim))
    g_scr[...] = jnp.log(f)
    k_scr[...] = 1.0 - f
    v_scr[...] = proj(base + 2 * fdim, fdim)

    nchunk = tm // HG_CHUNK
    nsub = HG_CHUNK // HG_SUB
    nt = (((1,), (1,)), ((), ()))
    tn = (((0,), (0,)), ((), ()))
    row = lax.broadcasted_iota(jnp.int32, (tm, tm), 0)
    col = lax.broadcasted_iota(jnp.int32, (tm, tm), 1)
    causal = (col <= row) & (col // HG_CHUNK == row // HG_CHUNK)
    gl = g_scr[...]
    g_hi = gl.astype(BF16)
    r_mid = gl - g_hi.astype(F32)
    g_mid = r_mid.astype(BF16)
    g_lo = (r_mid - g_mid.astype(F32)).astype(BF16)
    cs = jnp.dot(causal.astype(BF16), jnp.concatenate([g_hi, g_mid, g_lo], axis=1),
                 preferred_element_type=F32)
    b_scr[...] = cs[:, 0:fdim] + cs[:, fdim:2 * fdim] + cs[:, 2 * fdim:3 * fdim]
    ends = jnp.concatenate([b_scr[(c + 1) * HG_CHUNK - 1:(c + 1) * HG_CHUNK, :] for c in range(nchunk)], axis=0)
    max_decay = jnp.max(-ends)

    def fast_tile():
        bt = b_scr[...]
        kt = k_scr[...]
        q_hat = (q_scr[...] * jnp.exp(bt)).astype(BF16)
        k_bar = (kt * jnp.exp(-bt)).astype(BF16)
        vb = v_scr[...].astype(BF16)
        outs = []
        for hd in range(HG_HEADS):
            ls = slice(hd * HG_DK, (hd + 1) * HG_DK)
            sc = lax.dot_general(q_hat[:, ls], k_bar[:, ls], nt, preferred_element_type=F32)
            sc = jnp.where(causal, sc, 0.0).astype(BF16)
            intra = jnp.dot(sc, vb[:, ls], preferred_element_type=F32)
            st = st_scr[hd]
            inter = []
            for c in range(nchunk):
                rs = slice(c * HG_CHUNK, (c + 1) * HG_CHUNK)
                inter.append(lax.dot_general(q_hat[rs, ls], st.astype(BF16), nt, preferred_element_type=F32))
                b_last = bt[(c + 1) * HG_CHUNK - 1:(c + 1) * HG_CHUNK, ls]
                k_hat = (kt[rs, ls] * jnp.exp(b_last - bt[rs, ls])).astype(BF16)
                st = st * jnp.exp(b_last) + lax.dot_general(vb[rs, ls], k_hat, tn, preferred_element_type=F32)
            st_scr[hd] = st
            outs.append(intra + jnp.concatenate(inter, axis=0))
        o_scr[...] = jnp.concatenate(outs, axis=1)

    def safe_tile():
        srow = lax.broadcasted_iota(jnp.int32, (HG_CHUNK, HG_CHUNK), 0)
        scol = lax.broadcasted_iota(jnp.int32, (HG_CHUNK, HG_CHUNK), 1)
        offmask = scol < (srow // HG_SUB) * HG_SUB
        sub_row = lax.broadcasted_iota(jnp.int32, (HG_SUB, 1), 0)

        def chunk_body(c, carry):
            r0 = pl.multiple_of(c * HG_CHUNK, HG_CHUNK)
            rows = pl.ds(r0, HG_CHUNK)
            qc = q_scr[rows, :]
            kc = k_scr[rows, :]
            bc = b_scr[rows, :]
            b_last = bc[HG_CHUNK - 1:HG_CHUNK]
            q_hat = (qc * jnp.exp(bc)).astype(BF16)
            k_hat = (kc * jnp.exp(b_last - bc)).astype(BF16)
            dec = jnp.exp(b_last)
            vb = v_scr[rows, :].astype(BF16)
            q_t, k_t = [], []
            for blk in range(1, nsub):
                lo = blk * HG_SUB
                ref = bc[lo - 1:lo]
                q_t.append((qc[lo:lo + HG_SUB] * jnp.exp(bc[lo:lo + HG_SUB] - ref)).astype(BF16))
                k_t.append((kc * jnp.exp(jnp.minimum(ref - bc, 0.0))).astype(BF16))
            outs = []
            for hd in range(HG_HEADS):
                ls = slice(hd * HG_DK, (hd + 1) * HG_DK)
                st = st_scr[hd]
                inter = lax.dot_general(q_hat[:, ls], st.astype(BF16), nt, preferred_element_type=F32)
                sc = [jnp.zeros((HG_SUB, HG_CHUNK), F32)]
                for blk in range(1, nsub):
                    sc.append(lax.dot_general(q_t[blk - 1][:, ls], k_t[blk - 1][:, ls], nt,
                                              preferred_element_type=F32))
                scores = jnp.where(offmask, jnp.concatenate(sc, axis=0), 0.0).astype(BF16)
                outs.append(inter + jnp.dot(scores, vb[:, ls], preferred_element_type=F32))
                upd = lax.dot_general(vb[:, ls], k_hat[:, ls], tn, preferred_element_type=F32)
                st_scr[hd] = st * dec[:, ls] + upd
            o_scr[rows, :] = jnp.concatenate(outs, axis=1)

            def diag_body(s, carry2):
                lo = pl.multiple_of((s // HG_SUB) * HG_SUB, HG_SUB)
                bs = b_scr[pl.ds(r0 + s, 1), :]
                ks = k_scr[pl.ds(r0 + s, 1), :]
                vs = v_scr[pl.ds(r0 + s, 1), :]
                qi_ = q_scr[pl.ds(r0 + lo, HG_SUB), :]
                bi_ = b_scr[pl.ds(r0 + lo, HG_SUB), :]
                a = qi_ * jnp.exp(jnp.minimum(bi_ - bs, 0.0)) * ks
                keep = sub_row >= s - lo
                parts = []
                for hd in range(HG_HEADS):
                    ls = slice(hd * HG_DK, (hd + 1) * HG_DK)
                    rsum = jnp.where(keep, jnp.sum(a[:, ls], axis=-1, keepdims=True), 0.0)
                    parts.append(rsum * vs[:, ls])
                orow = pl.ds(r0 + lo, HG_SUB)
                o_scr[orow, :] = o_scr[orow, :] + jnp.concatenate(parts, axis=1)
                return carry2

            lax.fori_loop(0, HG_CHUNK, diag_body, 0)
            return carry

        lax.fori_loop(0, nchunk, chunk_body, 0)

    lax.cond(max_decay <= HG_FAST_MAX_EXP, fast_tile, safe_tile)

    o = o_scr[...]
    hgn = hgn_ref[...]
    parts = []
    for hd in range(HG_HEADS):
        ls = slice(hd * HG_DK, (hd + 1) * HG_DK)
        oh = o[:, ls]
        ms = jnp.mean(oh * oh, axis=-1, keepdims=True)
        parts.append(oh * lax.rsqrt(ms + EPS) * hgn[:, ls])
    hg = proj(base + 3 * fdim, fdim)
    y_b = jnp.concatenate(parts, axis=1) * (hg * jax.nn.sigmoid(hg))
    y_ref[0, :, cdim:cdim + fdim] = y_b.astype(y_ref.dtype)


def _l0_mix(x, mods, g, w_in, conv_w, hg_norm, lb_logits, *, layer, tm):
    bsz, s, d = x.shape
    n_in = w_in.shape[1]
    cdim = conv_w.shape[1]
    fdim = HG_HEADS * HG_DK
    kern = functools.partial(_l0_kernel, tm=tm, layer=layer)
    return pl.pallas_call(
        kern,
        grid=(bsz, s // tm),
        in_specs=[
            pl.BlockSpec((1, tm, d), lambda b, i: (b, i, 0)),
            pl.BlockSpec((1, 1, 6, d), lambda b, i: (layer, b, 0, 0)),
            pl.BlockSpec((1, d), lambda b, i: (0, 0)),
            pl.BlockSpec((d, n_in), lambda b, i: (0, 0)),
            pl.BlockSpec(conv_w.shape, lambda b, i: (0, 0)),
            pl.BlockSpec((1, fdim), lambda b, i: (0, 0)),
            pl.BlockSpec(lb_logits.shape, lambda b, i: (0, 0)),
        ],
        out_specs=pl.BlockSpec((1, tm, cdim + fdim), lambda b, i: (b, i, 0)),
        out_shape=jax.ShapeDtypeStruct((bsz, s, cdim + fdim), BF16),
        scratch_shapes=[
            pltpu.VMEM((tm, d), BF16),
            pltpu.VMEM((tm + 2 * SUBLANES, cdim), F32),
            pltpu.VMEM((tm, fdim), F32),
            pltpu.VMEM((tm, fdim), F32),
            pltpu.VMEM((tm, fdim), F32),
            pltpu.VMEM((tm, fdim), F32),
            pltpu.VMEM((tm, fdim), F32),
            pltpu.VMEM((tm, fdim), F32),
            pltpu.VMEM((HG_HEADS, HG_DK, HG_DK), F32),
        ],
        compiler_params=_cparams(("arbitrary", "arbitrary")),
        name="l0_mix",
    )(x, mods, g.reshape(1, d), w_in, conv_w, hg_norm.reshape(1, fdim), lb_logits)


def _post_kernel(x_ref, y_ref, mod_ref, g_ref, wo_ref, w1_ref, w2_ref, o_ref, *, transposed, ff_chunk):
    mod = mod_ref[0, 0]
    if transposed:
        y = jnp.concatenate([y_ref[0, j].astype(F32).T for j in range(y_ref.shape[1])], axis=0).astype(BF16)
    else:
        y = y_ref[0]
    x1 = x_ref[0] + mod[2:3] * jnp.dot(y, wo_ref[...], preferred_element_type=F32)
    hb = _norm_mod(x1, g_ref[...], mod[4:5], mod[3:4]).astype(BF16)
    acc = jnp.zeros(x1.shape, F32)
    for c0 in range(0, w1_ref.shape[1], ff_chunk):
        a = jnp.dot(hb, w1_ref[:, c0:c0 + ff_chunk], preferred_element_type=F32)
        a = jnp.square(jnp.maximum(a, 0.0)).astype(BF16)
        acc = acc + jnp.dot(a, w2_ref[c0:c0 + ff_chunk, :], preferred_element_type=F32)
    o_ref[0] = x1 + mod[5:6] * acc


def _post(x, y, mods, g, w_out, w1, w2, *, layer, tm, transposed):
    bsz, s, d = x.shape
    dff = w1.shape[1]
    if transposed:
        blk = y.shape[3]
        y_spec = pl.BlockSpec((1, tm // blk, y.shape[2], blk), lambda b, i: (b, i, 0, 0))
    else:
        y_spec = pl.BlockSpec((1, tm, y.shape[2]), lambda b, i: (b, i, 0))
    kern = functools.partial(_post_kernel, transposed=transposed, ff_chunk=min(dff, 1024))
    return pl.pallas_call(
        kern,
        grid=(bsz, s // tm),
        in_specs=[
            pl.BlockSpec((1, tm, d), lambda b, i: (b, i, 0)),
            y_spec,
            pl.BlockSpec((1, 1, 6, d), lambda b, i: (layer, b, 0, 0)),
            pl.BlockSpec((1, d), lambda b, i: (0, 0)),
            pl.BlockSpec(w_out.shape, lambda b, i: (0, 0)),
            pl.BlockSpec(w1.shape, lambda b, i: (0, 0)),
            pl.BlockSpec(w2.shape, lambda b, i: (0, 0)),
        ],
        out_specs=pl.BlockSpec((1, tm, d), lambda b, i: (b, i, 0)),
        out_shape=jax.ShapeDtypeStruct((bsz, s, d), F32),
        compiler_params=_cparams(("arbitrary", "arbitrary")),
        name="post_t" if transposed else "post",
    )(x, y, mods, g.reshape(1, d), w_out, w1, w2)


def _qkv_kernel(x_ref, mod_ref, g_ref, wq_ref, wkv_ref, qn_ref, kn_ref, o_ref, h_scr, *, tm, q_scale):
    mod = mod_ref[0, 0]
    h = _norm_mod(x_ref[0], g_ref[...], mod[1:2], mod[0:1])
    nlane = h_scr.shape[2]
    for cc in range(h_scr.shape[0]):
        h_scr[cc] = h[:, cc * nlane:(cc + 1) * nlane]
    nt = (((1,), (1,)), ((), ()))
    width = SB_HEADS * SB_HEAD_DIM
    seg = ATT_BLOCK // SUBLANES

    def head_norm(t, gain, mult):
        parts = []
        for hd in range(SB_HEADS):
            th = t[hd * SB_HEAD_DIM:(hd + 1) * SB_HEAD_DIM]
            ms = jnp.mean(th * th, axis=0, keepdims=True)
            parts.append(th * lax.rsqrt(ms + EPS) * gain * mult)
        return jnp.concatenate(parts, axis=0)

    q_t = lax.dot_general(wq_ref[...], h.astype(BF16), nt, preferred_element_type=F32)
    q_t = head_norm(q_t, qn_ref[...], q_scale).astype(o_ref.dtype)
    cols = []
    for cc in range(h_scr.shape[0]):
        pieces = []
        for blk in range(tm // ATT_BLOCK):
            for j in range(seg):
                pieces.append(h_scr[cc, pl.ds(blk * ATT_BLOCK + j, SUBLANES, stride=seg), :])
        cols.append(jnp.concatenate(pieces, axis=0))
    hp = jnp.concatenate(cols, axis=1).astype(BF16)
    kv_t = lax.dot_general(wkv_ref[...], hp, nt, preferred_element_type=F32)
    k_t = head_norm(kv_t[0:width], kn_ref[...], 1.0).astype(o_ref.dtype)
    v_t = kv_t[width:2 * width].astype(o_ref.dtype)
    for blk in range(tm // ATT_BLOCK):
        ls = slice(blk * ATT_BLOCK, (blk + 1) * ATT_BLOCK)
        o_ref[0, blk, 0:width, :] = q_t[:, ls]
        o_ref[0, blk, width:2 * width, :] = k_t[:, ls]
        o_ref[0, blk, 2 * width:3 * width, :] = v_t[:, ls]


def _qkv(x, mods, g, wq_t, wkv_t, q_norm, k_norm, *, layer, tm):
    bsz, s, d = x.shape
    width = SB_HEADS * SB_HEAD_DIM
    q_scale = 0.5 * SB_HEAD_DIM ** -0.5
    kern = functools.partial(_qkv_kernel, tm=tm, q_scale=q_scale)
    return pl.pallas_call(
        kern,
        grid=(bsz, s // tm),
        in_specs=[
            pl.BlockSpec((1, tm, d), lambda b, i: (b, i, 0)),
            pl.BlockSpec((1, 1, 6, d), lambda b, i: (layer, b, 0, 0)),
            pl.BlockSpec((1, d), lambda b, i: (0, 0)),
            pl.BlockSpec(wq_t.shape, lambda b, i: (0, 0)),
            pl.BlockSpec(wkv_t.shape, lambda b, i: (0, 0)),
            pl.BlockSpec((SB_HEAD_DIM, 1), lambda b, i: (0, 0)),
            pl.BlockSpec((SB_HEAD_DIM, 1), lambda b, i: (0, 0)),
        ],
        out_specs=pl.BlockSpec((1, tm // ATT_BLOCK, 3 * width, ATT_BLOCK), lambda b, i: (b, i, 0, 0)),
        out_shape=jax.ShapeDtypeStruct((bsz, s // ATT_BLOCK, 3 * width, ATT_BLOCK), BF16),
        scratch_shapes=[pltpu.VMEM((d // LANES, tm, LANES), F32)],
        compiler_params=_cparams(("arbitrary", "arbitrary")),
        name="qkv",
    )(x, mods, g.reshape(1, d), wq_t, wkv_t, q_norm.reshape(SB_HEAD_DIM, 1), k_norm.reshape(SB_HEAD_DIM, 1))


def _attn_kernel(q_ref, k_ref, v_ref, o_ref, z_scr, w1_scr, w_scr, acc_scr, car_scr, *, heads):
    qi = pl.program_id(2)
    blk = ATT_BLOCK
    seg = blk // SUBLANES
    dh = SB_HEAD_DIM
    tn = (((0,), (0,)), ((), ()))
    acc_scr[...] = jnp.zeros_like(acc_scr)
    car_scr[...] = jnp.ones_like(car_scr)
    sub = lax.broadcasted_iota(jnp.int32, (SUBLANES, blk), 0)
    lane = lax.broadcasted_iota(jnp.int32, (SUBLANES, blk), 1)
    ones = jnp.ones((SUBLANES, blk), F32)

    def scores(g, kb):
        hs = slice(g * dh, (g + 1) * dh)
        return lax.dot_general(k_ref[0, kb, hs, :], q_ref[0, 0, hs, :], tn, preferred_element_type=F32)

    for g in range(ATT_QK_AHEAD):
        z_scr[g] = scores(g, qi)

    def process(kblk, masked):
        zs = {}
        for g in range(heads):
            hs = slice(g * dh, (g + 1) * dh)
            if g + ATT_QK_AHEAD < heads:
                zs[g + ATT_QK_AHEAD] = scores(g + ATT_QK_AHEAD, kblk)
            if g < ATT_QK_AHEAD:
                z = z_scr[g]
            else:
                z = zs.pop(g)
            if g == ATT_QK_AHEAD:
                for gn in range(ATT_QK_AHEAD):
                    z_scr[gn] = scores(gn, jnp.maximum(kblk - 1, 0))
            c = ones
            for m in reversed(range(seg // 2)):
                pair = []
                for j in (2 * m + 1, 2 * m):
                    beta = 0.5 * jnp.tanh(z[j * SUBLANES:(j + 1) * SUBLANES, :]) + 0.5
                    if masked:
                        beta = jnp.where((sub * seg + j) < lane, beta, 0.0)
                    w = beta * c
                    pair.append(w)
                    c = c - w
                w1_scr[g, m * 2 * SUBLANES:(m + 1) * 2 * SUBLANES, :] = (
                    jnp.concatenate([pair[1], pair[0]], axis=0).astype(w1_scr.dtype))
            inc = c
            for d in (1, 2, 4):
                sh = pltpu.roll(inc, SUBLANES - d, axis=0)
                inc = inc * jnp.where(sub + d < SUBLANES, sh, 1.0)
            excl = jnp.where(sub + 1 < SUBLANES, pltpu.roll(inc, SUBLANES - 1, axis=0), 1.0)
            carry = car_scr[g]
            scale = excl * carry
            car_scr[g] = carry * jnp.broadcast_to(inc[0:1], (SUBLANES, blk))
            scale2 = jnp.concatenate([scale, scale], axis=0).astype(w_scr.dtype)
            for m in range(seg // 2):
                rows = slice(m * 2 * SUBLANES, (m + 1) * 2 * SUBLANES)
                w_scr[g, rows, :] = w1_scr[g, rows, :] * scale2
            acc_scr[hs, :] += jnp.dot(v_ref[0, kblk, hs, :], w_scr[g], preferred_element_type=F32)

    process(qi, True)

    def body(n, carry):
        process(qi - 1 - 2 * n, False)
        process(qi - 2 - 2 * n, False)
        return carry

    lax.fori_loop(0, qi // 2, body, 0)

    @pl.when(qi % 2 == 1)
    def _():
        process(0, False)
    o_ref[0, 0] = acc_scr[...].astype(o_ref.dtype)


def _attn(qkv_t, *, heads):
    bsz, ns, rows, blk = qkv_t.shape
    width = SB_HEADS * SB_HEAD_DIM
    ngroup = SB_HEADS // heads
    hrows = heads * SB_HEAD_DIM
    return pl.pallas_call(
        functools.partial(_attn_kernel, heads=heads),
        grid=(bsz, ngroup, ns),
        in_specs=[
            pl.BlockSpec((1, 1, hrows, blk), lambda b, h, i: (b, i, h, 0)),
            pl.BlockSpec((1, ns, hrows, blk), lambda b, h, i: (b, 0, ngroup + h, 0)),
            pl.BlockSpec((1, ns, hrows, blk), lambda b, h, i: (b, 0, 2 * ngroup + h, 0)),
        ],
        out_specs=pl.BlockSpec((1, 1, hrows, blk), lambda b, h, i: (b, i, h, 0)),
        out_shape=jax.ShapeDtypeStruct((bsz, ns, width, blk), BF16),
        scratch_shapes=[
            pltpu.VMEM((ATT_QK_AHEAD, blk, blk), F32),
            pltpu.VMEM((heads, blk, blk), BF16),
            pltpu.VMEM((heads, blk, blk), BF16),
            pltpu.VMEM((hrows, blk), F32),
            pltpu.VMEM((heads, SUBLANES, blk), F32),
        ],
        compiler_params=_cparams(("arbitrary", "arbitrary", "arbitrary")),
        name="attn",
    )(qkv_t, qkv_t, qkv_t)


def kernel(x, c, ada_w, ada_b, norm_mix, norm_mlp, w_in_ab, conv_w, hg_norm, lb_logits, w_out_ab,
           w_qkv, q_norm, k_norm, w_out_c, mlp_w1, mlp_w2):
    bsz, s, d = x.shape
    depth = ada_w.shape[0]
    tm = min(s, 512)
    width = SB_HEADS * SB_HEAD_DIM
    mods = _ada(c, ada_w, ada_b).reshape(depth, bsz, 6, d)
    for layer in range(depth):
        j = layer // 2
        w1 = mlp_w1[layer].astype(BF16)
        w2 = mlp_w2[layer].astype(BF16)
        if layer % 2 == 0:
            y = _l0_mix(x, mods, norm_mix[layer], w_in_ab[j].astype(BF16), conv_w[j], hg_norm[j],
                        lb_logits, layer=layer, tm=min(s, 256))
            x = _post(x, y, mods, norm_mlp[layer], w_out_ab[j].astype(BF16), w1, w2,
                      layer=layer, tm=tm, transposed=False)
        else:
            wq_t = w_qkv[j][:, 0:width].T.astype(BF16)
            wkv_t = w_qkv[j][:, width:3 * width].T.astype(BF16)
            qkv_t = _qkv(x, mods, norm_mix[layer], wq_t, wkv_t, q_norm[j], k_norm[j], layer=layer, tm=tm)
            o_t = _attn(qkv_t, heads=ATT_HEADS_PER_STEP)
            x = _post(x, o_t, mods, norm_mlp[layer], w_out_c[j].astype(BF16), w1, w2,
                      layer=layer, tm=tm, transposed=True)
    return x
```

```python
import functools

import jax
import jax.numpy as jnp
from jax import lax
from jax.experimental import pallas as pl
from jax.experimental.pallas import tpu as pltpu

F32 = jnp.float32
BF16 = jnp.bfloat16
HIGHEST = lax.Precision.HIGHEST

EPS = 1e-6
HG_CHUNK = 64
HG_SUB = 16
HG_FAST_MAX_EXP = 60.0
HG_HEADS = 4
HG_DK = 128
SB_HEADS = 16
SB_HEAD_DIM = 64
ATT_BLOCK = 256
ATT_HEADS_PER_STEP = 16
ATT_QK_AHEAD = 8
SUBLANES = 8
LANES = 128
VMEM_LIMIT = 56 * 1024 * 1024


def _cparams(sem):
    return pltpu.CompilerParams(dimension_semantics=sem, vmem_limit_bytes=VMEM_LIMIT)


def _norm_mod(x, g, scale, shift):
    ms = jnp.mean(x * x, axis=-1, keepdims=True)
    return (x * lax.rsqrt(ms + EPS) * g) * (1.0 + scale) + shift


def _ada_kernel(c_ref, w_ref, b_ref, o_ref):
    c = c_ref[...]
    ca = c * jax.nn.sigmoid(c)
    o_ref[0] = jnp.dot(ca, w_ref[0], preferred_element_type=F32, precision=HIGHEST) + b_ref[0]


def _ada(c, ada_w, ada_b):
    depth, d, n = ada_w.shape
    bsz = c.shape[0]
    tn = 1536 if n % 1536 == 0 else n
    return pl.pallas_call(
        _ada_kernel,
        grid=(depth, n // tn),
        in_specs=[
            pl.BlockSpec((bsz, d), lambda l, j: (0, 0)),
            pl.BlockSpec((1, d, tn), lambda l, j: (l, 0, j)),
            pl.BlockSpec((1, 1, tn), lambda l, j: (l, 0, j)),
        ],
        out_specs=pl.BlockSpec((1, bsz, tn), lambda l, j: (l, 0, j)),
        out_shape=jax.ShapeDtypeStruct((depth, bsz, n), F32),
        compiler_params=_cparams(("arbitrary", "arbitrary")),
        name="ada",
    )(c, ada_w, ada_b.reshape(depth, 1, n))


def _l0_kernel(x_ref, mod_ref, g_ref, w_ref, cw_ref, hgn_ref, lb_ref, y_ref,
               hb_scr, p_scr, q_scr, k_scr, v_scr, g_scr, b_scr, o_scr, st_scr,
               *, tm, layer):
    i = pl.program_id(1)
    cdim = cw_ref.shape[1]
    fdim = HG_HEADS * HG_DK

    @pl.when(i == 0)
    def _():
        st_scr[...] = jnp.zeros_like(st_scr)
        p_scr[0:SUBLANES, :] = jnp.zeros((SUBLANES, cdim), F32)

    mod = mod_ref[0, 0]
    h = _norm_mod(x_ref[0], g_ref[...], mod[1:2], mod[0:1])
    hb_scr[...] = h.astype(BF16)

    def proj(j, width):
        return jnp.dot(hb_scr[...], w_ref[:, j:j + width], preferred_element_type=F32)

    a_b = proj(0, cdim)
    p = proj(cdim, cdim) * proj(2 * cdim, cdim)
    p_scr[SUBLANES:SUBLANES + tm, :] = p
    p1 = p_scr[SUBLANES - 1:SUBLANES - 1 + tm, :]
    p2 = p_scr[SUBLANES - 2:SUBLANES - 2 + tm, :]
    cw = cw_ref[...]
    y_a = a_b * (cw[0:1] * p2 + cw[1:2] * p1 + cw[2:3] * p)
    p_scr[0:SUBLANES, :] = p_scr[tm:tm + SUBLANES, :]
    y_ref[0, :, 0:cdim] = y_a.astype(y_ref.dtype)

    base = 3 * cdim
    lbl = lb_ref[...]
    e = jnp.exp(lbl - jnp.max(lbl, axis=0, keepdims=True))
    lb = jnp.sum(e[0:layer + 1], axis=0, keepdims=True) / jnp.sum(e, axis=0, keepdims=True)
    q_scr[...] = proj(base, fdim)
    f = lb + (1.0 - lb) * jax.nn.sigmoid(proj(base + fdim, fdim))
    g_scr[...] = jnp.log(f)
    k_scr[...] = 1.0 - f
    v_scr[...] = proj(base + 2 * fdim, fdim)

    nchunk = tm // HG_CHUNK
    nsub = HG_CHUNK // HG_SUB
    nt = (((1,), (1,)), ((), ()))
    tn = (((0,), (0,)), ((), ()))
    row = lax.broadcasted_iota(jnp.int32, (tm, tm), 0)
    col = lax.broadcasted_iota(jnp.int32, (tm, tm), 1)
    causal = (col <= row) & (col // HG_CHUNK == row // HG_CHUNK)
    gl = g_scr[...]
    g_hi = gl.astype(BF16)
    r_mid = gl - g_hi.astype(F32)
    g_mid = r_mid.astype(BF16)
    g_lo = (r_mid - g_mid.astype(F32)).astype(BF16)
    cs = jnp.dot(causal.astype(BF16), jnp.concatenate([g_hi, g_mid, g_lo], axis=1),
                 preferred_element_type=F32)
    b_scr[...] = cs[:, 0:fdim] + cs[:, fdim:2 * fdim] + cs[:, 2 * fdim:3 * fdim]
    ends = jnp.concatenate([b_scr[(c + 1) * HG_CHUNK - 1:(c + 1) * HG_CHUNK, :] for c in range(nchunk)], axis=0)
    max_decay = jnp.max(-ends)

    def fast_tile():
        bt = b_scr[...]
        kt = k_scr[...]
        q_hat = (q_scr[...] * jnp.exp(bt)).astype(BF16)
        k_bar = (kt * jnp.exp(-bt)).astype(BF16)
        vb = v_scr[...].astype(BF16)
        outs = []
        for hd in range(HG_HEADS):
            ls = slice(hd * HG_DK, (hd + 1) * HG_DK)
            sc = lax.dot_general(q_hat[:, ls], k_bar[:, ls], nt, preferred_element_type=F32)
            sc = jnp.where(causal, sc, 0.0).astype(BF16)
            intra = jnp.dot(sc, vb[:, ls], preferred_element_type=F32)
            st = st_scr[hd]
            inter = []
            for c in range(nchunk):
                rs = slice(c * HG_CHUNK, (c + 1) * HG_CHUNK)
                inter.append(lax.dot_general(q_hat[rs, ls], st.astype(BF16), nt, preferred_element_type=F32))
                b_last = bt[(c + 1) * HG_CHUNK - 1:(c + 1) * HG_CHUNK, ls]
                k_hat = (kt[rs, ls] * jnp.exp(b_last - bt[rs, ls])).astype(BF16)
                st = st * jnp.exp(b_last) + lax.dot_general(vb[rs, ls], k_hat, tn, preferred_element_type=F32)
            st_scr[hd] = st
            outs.append(intra + jnp.concatenate(inter, axis=0))
        o_scr[...] = jnp.concatenate(outs, axis=1)

    def safe_tile():
        srow = lax.broadcasted_iota(jnp.int32, (HG_CHUNK, HG_CHUNK), 0)
        scol = lax.broadcasted_iota(jnp.int32, (HG_CHUNK, HG_CHUNK), 1)
        offmask = scol < (srow // HG_SUB) * HG_SUB
        sub_row = lax.broadcasted_iota(jnp.int32, (HG_SUB, 1), 0)

        def chunk_body(c, carry):
            r0 = pl.multiple_of(c * HG_CHUNK, HG_CHUNK)
            rows = pl.ds(r0, HG_CHUNK)
            qc = q_scr[rows, :]
            kc = k_scr[rows, :]
            bc = b_scr[rows, :]
            b_last = bc[HG_CHUNK - 1:HG_CHUNK]
            q_hat = (qc * jnp.exp(bc)).astype(BF16)
            k_hat = (kc * jnp.exp(b_last - bc)).astype(BF16)
            dec = jnp.exp(b_last)
            vb = v_scr[rows, :].astype(BF16)
            q_t, k_t = [], []
            for blk in range(1, nsub):
                lo = blk * HG_SUB
                ref = bc[lo - 1:lo]
                q_t.append((qc[lo:lo + HG_SUB] * jnp.exp(bc[lo:lo + HG_SUB] - ref)).astype(BF16))
                k_t.append((kc * jnp.exp(jnp.minimum(ref - bc, 0.0))).astype(BF16))
            outs = []
            for hd in range(HG_HEADS):
                ls = slice(hd * HG_DK, (hd + 1) * HG_DK)
                st = st_scr[hd]
                inter = lax.dot_general(q_hat[:, ls], st.astype(BF16), nt, preferred_element_type=F32)
                sc = [jnp.zeros((HG_SUB, HG_CHUNK), F32)]
                for blk in range(1, nsub):
                    sc.append(lax.dot_general(q_t[blk - 1][:, ls], k_t[blk - 1][:, ls], nt,
                                              preferred_element_type=F32))
                scores = jnp.where(offmask, jnp.concatenate(sc, axis=0), 0.0).astype(BF16)
                outs.append(inter + jnp.dot(scores, vb[:, ls], preferred_element_type=F32))
                upd = lax.dot_general(vb[:, ls], k_hat[:, ls], tn, preferred_element_type=F32)
                st_scr[hd] = st * dec[:, ls] + upd
            o_scr[rows, :] = jnp.concatenate(outs, axis=1)

            def diag_body(s, carry2):
                lo = pl.multiple_of((s // HG_SUB) * HG_SUB, HG_SUB)
                bs = b_scr[pl.ds(r0 + s, 1), :]
                ks = k_scr[pl.ds(r0 + s, 1), :]
                vs = v_scr[pl.ds(r0 + s, 1), :]
                qi_ = q_scr[pl.ds(r0 + lo, HG_SUB), :]
                bi_ = b_scr[pl.ds(r0 + lo, HG_SUB), :]
                a = qi_ * jnp.exp(jnp.minimum(bi_ - bs, 0.0)) * ks
                keep = sub_row >= s - lo
                parts = []
                for hd in range(HG_HEADS):
                    ls = slice(hd * HG_DK, (hd + 1) * HG_DK)
                    rsum = jnp.where(keep, jnp.sum(a[:, ls], axis=-1, keepdims=True), 0.0)
                    parts.append(rsum * vs[:, ls])
                orow = pl.ds(r0 + lo, HG_SUB)
                o_scr[orow, :] = o_scr[orow, :] + jnp.concatenate(parts, axis=1)
                return carry2

            lax.fori_loop(0, HG_CHUNK, diag_body, 0)
            return carry

        lax.fori_loop(0, nchunk, chunk_body, 0)

    lax.cond(max_decay <= HG_FAST_MAX_EXP, fast_tile, safe_tile)

    o = o_scr[...]
    hgn = hgn_ref[...]
    parts = []
    for hd in range(HG_HEADS):
        ls = slice(hd * HG_DK, (hd + 1) * HG_DK)
        oh = o[:, ls]
        ms = jnp.mean(oh * oh, axis=-1, keepdims=True)
        parts.append(oh * lax.rsqrt(ms + EPS) * hgn[:, ls])
    hg = proj(base + 3 * fdim, fdim)
    y_b = jnp.concatenate(parts, axis=1) * (hg * jax.nn.sigmoid(hg))
    y_ref[0, :, cdim:cdim + fdim] = y_b.astype(y_ref.dtype)


def _l0_mix(x, mods, g, w_in, conv_w, hg_norm, lb_logits, *, layer, tm):
    bsz, s, d = x.shape
    n_in = w_in.shape[1]
    cdim = conv_w.shape[1]
    fdim = HG_HEADS * HG_DK
    kern = functools.partial(_l0_kernel, tm=tm, layer=layer)
    return pl.pallas_call(
        kern,
        grid=(bsz, s // tm),
        in_specs=[
            pl.BlockSpec((1, tm, d), lambda b, i: (b, i, 0)),
            pl.BlockSpec((1, 1, 6, d), lambda b, i: (layer, b, 0, 0)),
            pl.BlockSpec((1, d), lambda b, i: (0, 0)),
            pl.BlockSpec((d, n_in), lambda b, i: (0, 0)),
            pl.BlockSpec(conv_w.shape, lambda b, i: (0, 0)),
            pl.BlockSpec((1, fdim), lambda b, i: (0, 0)),
            pl.BlockSpec(lb_logits.shape, lambda b, i: (0, 0)),
        ],
        out_specs=pl.BlockSpec((1, tm, cdim + fdim), lambda b, i: (b, i, 0)),
        out_shape=jax.ShapeDtypeStruct((bsz, s, cdim + fdim), BF16),
        scratch_shapes=[
            pltpu.VMEM((tm, d), BF16),
            pltpu.VMEM((tm + 2 * SUBLANES, cdim), F32),
            pltpu.VMEM((tm, fdim), F32),
            pltpu.VMEM((tm, fdim), F32),
            pltpu.VMEM((tm, fdim), F32),
            pltpu.VMEM((tm, fdim), F32),
            pltpu.VMEM((tm, fdim), F32),
            pltpu.VMEM((tm, fdim), F32),
            pltpu.VMEM((HG_HEADS, HG_DK, HG_DK), F32),
        ],
        compiler_params=_cparams(("arbitrary", "arbitrary")),
        name="l0_mix",
    )(x, mods, g.reshape(1, d), w_in, conv_w, hg_norm.reshape(1, fdim), lb_logits)


def _post_kernel(x_ref, y_ref, mod_ref, g_ref, wo_ref, w1_ref, w2_ref, o_ref, *, transposed, ff_chunk):
    mod = mod_ref[0, 0]
    if transposed:
        y = jnp.concatenate([y_ref[0, j].astype(F32).T for j in range(y_ref.shape[1])], axis=0).astype(BF16)
    else:
        y = y_ref[0]
    x1 = x_ref[0] + mod[2:3] * jnp.dot(y, wo_ref[...], preferred_element_type=F32)
    hb = _norm_mod(x1, g_ref[...], mod[4:5], mod[3:4]).astype(BF16)
    acc = jnp.zeros(x1.shape, F32)
    for c0 in range(0, w1_ref.shape[1], ff_chunk):
        a = jnp.dot(hb, w1_ref[:, c0:c0 + ff_chunk], preferred_element_type=F32)
        a = jnp.square(jnp.maximum(a, 0.0)).astype(BF16)
        acc = acc + jnp.dot(a, w2_ref[c0:c0 + ff_chunk, :], preferred_element_type=F32)
    o_ref[0] = x1 + mod[5:6] * acc


def _post(x, y, mods, g, w_out, w1, w2, *, layer, tm, transposed):
    bsz, s, d = x.shape
    dff = w1.shape[1]
    if transposed:
        blk = y.shape[3]
        y_spec = pl.BlockSpec((1, tm // blk, y.shape[2], blk), lambda b, i: (b, i, 0, 0))
    else:
        y_spec = pl.BlockSpec((1, tm, y.shape[2]), lambda b, i: (b, i, 0))
    kern = functools.partial(_post_kernel, transposed=transposed, ff_chunk=min(dff, 1024))
    return pl.pallas_call(
        kern,
        grid=(bsz, s // tm),
        in_specs=[
            pl.BlockSpec((1, tm, d), lambda b, i: (b, i, 0)),
            y_spec,
            pl.BlockSpec((1, 1, 6, d), lambda b, i: (layer, b, 0, 0)),
            pl.BlockSpec((1, d), lambda b, i: (0, 0)),
            pl.BlockSpec(w_out.shape, lambda b, i: (0, 0)),
            pl.BlockSpec(w1.shape, lambda b, i: (0, 0)),
            pl.BlockSpec(w2.shape, lambda b, i: (0, 0)),
        ],
        out_specs=pl.BlockSpec((1, tm, d), lambda b, i: (b, i, 0)),
        out_shape=jax.ShapeDtypeStruct((bsz, s, d), F32),
        compiler_params=_cparams(("arbitrary", "arbitrary")),
        name="post_t" if transposed else "post",
    )(x, y, mods, g.reshape(1, d), w_out, w1, w2)


def _qkv_kernel(x_ref, mod_ref, g_ref, wq_ref, wkv_ref, qn_ref, kn_ref, o_ref, h_scr, *, tm, q_scale):
    mod = mod_ref[0, 0]
    h = _norm_mod(x_ref[0], g_ref[...], mod[1:2], mod[0:1])
    nlane = h_scr.shape[2]
    for cc in range(h_scr.shape[0]):
        h_scr[cc] = h[:, cc * nlane:(cc + 1) * nlane]
    nt = (((1,), (1,)), ((), ()))
    width = SB_HEADS * SB_HEAD_DIM
    seg = ATT_BLOCK // SUBLANES

    def head_norm(t, gain, mult):
        parts = []
        for hd in range(SB_HEADS):
            th = t[hd * SB_HEAD_DIM:(hd + 1) * SB_HEAD_DIM]
            ms = jnp.mean(th * th, axis=0, keepdims=True)
            parts.append(th * lax.rsqrt(ms + EPS) * gain * mult)
        return jnp.concatenate(parts, axis=0)

    q_t = lax.dot_general(wq_ref[...], h.astype(BF16), nt, preferred_element_type=F32)
    q_t = head_norm(q_t, qn_ref[...], q_scale).astype(o_ref.dtype)
    cols = []
    for cc in range(h_scr.shape[0]):
        pieces = []
        for blk in range(tm // ATT_BLOCK):
            for j in range(seg):
                pieces.append(h_scr[cc, pl.ds(blk * ATT_BLOCK + j, SUBLANES, stride=seg), :])
        cols.append(jnp.concatenate(pieces, axis=0))
    hp = jnp.concatenate(cols, axis=1).astype(BF16)
    kv_t = lax.dot_general(wkv_ref[...], hp, nt, preferred_element_type=F32)
    k_t = head_norm(kv_t[0:width], kn_ref[...], 1.0).astype(o_ref.dtype)
    pos = lax.broadcasted_iota(jnp.int32, (1, tm), 1)
    expo = ((pos % ATT_BLOCK) // SUBLANES) - seg
    v_scale = lax.bitcast_convert_type((expo + 127) << 23, F32)
    v_t = (kv_t[width:2 * width] * v_scale).astype(o_ref.dtype)
    for blk in range(tm // ATT_BLOCK):
        ls = slice(blk * ATT_BLOCK, (blk + 1) * ATT_BLOCK)
        o_ref[0, blk, 0:width, :] = q_t[:, ls]
        o_ref[0, blk, width:2 * width, :] = k_t[:, ls]
        o_ref[0, blk, 2 * width:3 * width, :] = v_t[:, ls]


def _qkv(x, mods, g, wq_t, wkv_t, q_norm, k_norm, *, layer, tm):
    bsz, s, d = x.shape
    width = SB_HEADS * SB_HEAD_DIM
    q_scale = 0.5 * SB_HEAD_DIM ** -0.5
    kern = functools.partial(_qkv_kernel, tm=tm, q_scale=q_scale)
    return pl.pallas_call(
        kern,
        grid=(bsz, s // tm),
        in_specs=[
            pl.BlockSpec((1, tm, d), lambda b, i: (b, i, 0)),
            pl.BlockSpec((1, 1, 6, d), lambda b, i: (layer, b, 0, 0)),
            pl.BlockSpec((1, d), lambda b, i: (0, 0)),
            pl.BlockSpec(wq_t.shape, lambda b, i: (0, 0)),
            pl.BlockSpec(wkv_t.shape, lambda b, i: (0, 0)),
            pl.BlockSpec((SB_HEAD_DIM, 1), lambda b, i: (0, 0)),
            pl.BlockSpec((SB_HEAD_DIM, 1), lambda b, i: (0, 0)),
        ],
        out_specs=pl.BlockSpec((1, tm // ATT_BLOCK, 3 * width, ATT_BLOCK), lambda b, i: (b, i, 0, 0)),
        out_shape=jax.ShapeDtypeStruct((bsz, s // ATT_BLOCK, 3 * width, ATT_BLOCK), BF16),
        scratch_shapes=[pltpu.VMEM((d // LANES, tm, LANES), F32)],
        compiler_params=_cparams(("arbitrary", "arbitrary")),
        name="qkv",
    )(x, mods, g.reshape(1, d), wq_t, wkv_t, q_norm.reshape(SB_HEAD_DIM, 1), k_norm.reshape(SB_HEAD_DIM, 1))


def _attn_kernel(q_ref, k_ref, v_ref, o_ref, z_scr, w1_scr, w_scr, acc_scr, car_scr, *, heads):
    qi = pl.program_id(2)
    blk = ATT_BLOCK
    seg = blk // SUBLANES
    dh = SB_HEAD_DIM
    tn = (((0,), (0,)), ((), ()))
    acc_scr[...] = jnp.zeros_like(acc_scr)
    car_scr[...] = jnp.ones_like(car_scr)
    sub = lax.broadcasted_iota(jnp.int32, (SUBLANES, blk), 0)
    lane = lax.broadcasted_iota(jnp.int32, (SUBLANES, blk), 1)
    ones = jnp.ones((SUBLANES, blk), F32)

    def scores(g, kb):
        hs = slice(g * dh, (g + 1) * dh)
        return lax.dot_general(k_ref[0, kb, hs, :], q_ref[0, 0, hs, :], tn, preferred_element_type=F32)

    for g in range(ATT_QK_AHEAD):
        z_scr[g] = scores(g, qi)

    def process(kblk, masked):
        zs = {}
        for g in range(heads):
            hs = slice(g * dh, (g + 1) * dh)
            if g + ATT_QK_AHEAD < heads:
                zs[g + ATT_QK_AHEAD] = scores(g + ATT_QK_AHEAD, kblk)
            z = z_scr.at[g] if g < ATT_QK_AHEAD else zs.pop(g)
            if g == ATT_QK_AHEAD:
                for gn in range(ATT_QK_AHEAD):
                    z_scr[gn] = scores(gn, jnp.maximum(kblk - 1, 0))
            cs = ones
            for m in reversed(range(seg // 2)):
                pair = []
                for j in (2 * m + 1, 2 * m):
                    t = jnp.tanh(z[j * SUBLANES:(j + 1) * SUBLANES, :])
                    if masked:
                        t = jnp.where((sub * seg + j) < lane, t, -1.0)
                    x = cs * t
                    pair.append(cs + x)
                    cs = cs - x
                w1_scr[g, m * 2 * SUBLANES:(m + 1) * 2 * SUBLANES, :] = (
                    jnp.concatenate([pair[1], pair[0]], axis=0).astype(w1_scr.dtype))
            c = cs * (2.0 ** -seg)
            inc = c
            for d in (1, 2, 4):
                sh = pltpu.roll(inc, SUBLANES - d, axis=0)
                inc = inc * jnp.where(sub + d < SUBLANES, sh, 1.0)
            excl = jnp.where(sub + 1 < SUBLANES, pltpu.roll(inc, SUBLANES - 1, axis=0), 1.0)
            carry = car_scr[g]
            scale = excl * carry
            car_scr[g] = carry * jnp.broadcast_to(inc[0:1], (SUBLANES, blk))
            scale2 = jnp.concatenate([scale, scale], axis=0).astype(w_scr.dtype)
            for m in range(seg // 2):
                rows = slice(m * 2 * SUBLANES, (m + 1) * 2 * SUBLANES)
                w_scr[g, rows, :] = w1_scr[g, rows, :] * scale2
            acc_scr[hs, :] += jnp.dot(v_ref[0, kblk, hs, :], w_scr[g], preferred_element_type=F32)

    process(qi, True)

    def body(n, carry):
        process(qi - 1 - 2 * n, False)
        process(qi - 2 - 2 * n, False)
        return carry

    lax.fori_loop(0, qi // 2, body, 0)

    @pl.when(qi % 2 == 1)
    def _():
        process(0, False)
    o_ref[0, 0] = acc_scr[...].astype(o_ref.dtype)


def _attn(qkv_t, *, heads):
    bsz, ns, rows, blk = qkv_t.shape
    width = SB_HEADS * SB_HEAD_DIM
    ngroup = SB_HEADS // heads
    hrows = heads * SB_HEAD_DIM
    return pl.pallas_call(
        functools.partial(_attn_kernel, heads=heads),
        grid=(bsz, ngroup, ns),
        in_specs=[
            pl.BlockSpec((1, 1, hrows, blk), lambda b, h, i: (b, i, h, 0)),
            pl.BlockSpec((1, ns, hrows, blk), lambda b, h, i: (b, 0, ngroup + h, 0)),
            pl.BlockSpec((1, ns, hrows, blk), lambda b, h, i: (b, 0, 2 * ngroup + h, 0)),
        ],
        out_specs=pl.BlockSpec((1, 1, hrows, blk), lambda b, h, i: (b, i, h, 0)),
        out_shape=jax.ShapeDtypeStruct((bsz, ns, width, blk), BF16),
        scratch_shapes=[
            pltpu.VMEM((ATT_QK_AHEAD, blk, blk), F32),
            pltpu.VMEM((heads, blk, blk), BF16),
            pltpu.VMEM((heads, blk, blk), BF16),
            pltpu.VMEM((hrows, blk), F32),
            pltpu.VMEM((heads, SUBLANES, blk), F32),
        ],
        compiler_params=_cparams(("arbitrary", "arbitrary", "arbitrary")),
        name="attn",
    )(qkv_t, qkv_t, qkv_t)


def kernel(x, c, ada_w, ada_b, norm_mix, norm_mlp, w_in_ab, conv_w, hg_norm, lb_logits, w_out_ab,
           w_qkv, q_norm, k_norm, w_out_c, mlp_w1, mlp_w2):
    bsz, s, d = x.shape
    depth = ada_w.shape[0]
    tm = min(s, 512)
    width = SB_HEADS * SB_HEAD_DIM
    mods = _ada(c, ada_w, ada_b).reshape(depth, bsz, 6, d)
    for layer in range(depth):
        j = layer // 2
        w1 = mlp_w1[layer].astype(BF16)
        w2 = mlp_w2[layer].astype(BF16)
        if layer % 2 == 0:
            y = _l0_mix(x, mods, norm_mix[layer], w_in_ab[j].astype(BF16), conv_w[j], hg_norm[j],
                        lb_logits, layer=layer, tm=min(s, 256))
            x = _post(x, y, mods, norm_mlp[layer], w_out_ab[j].astype(BF16), w1, w2,
                      layer=layer, tm=tm, transposed=False)
        else:
            wq_t = w_qkv[j][:, 0:width].T.astype(BF16)
            wkv_t = w_qkv[j][:, width:3 * width].T.astype(BF16)
            qkv_t = _qkv(x, mods, norm_mix[layer], wq_t, wkv_t, q_norm[j], k_norm[j], layer=layer, tm=tm)
            o_t = _attn(qkv_t, heads=ATT_HEADS_PER_STEP)
            x = _post(x, o_t, mods, norm_mlp[layer], w_out_c[j].astype(BF16), w1, w2,
                      layer=layer, tm=tm, transposed=True)
    return x
```

```python
import functools

import jax
import jax.numpy as jnp
from jax import lax
from jax.experimental import pallas as pl
from jax.experimental.pallas import tpu as pltpu

F32 = jnp.float32
BF16 = jnp.bfloat16
HIGHEST = lax.Precision.HIGHEST

EPS = 1e-6
HG_CHUNK = 64
HG_SUB = 16
HG_FAST_MAX_EXP = 60.0
HG_TILE = 256
HG_HEADS = 4
HG_DK = 128
SB_HEADS = 16
SB_HEAD_DIM = 64
ATT_BLOCK = 256
ATT_HEADS_PER_STEP = 16
ATT_QK_AHEAD = 8
SUBLANES = 8
LANES = 128
VMEM_LIMIT = 56 * 1024 * 1024


def _cparams(sem):
    return pltpu.CompilerParams(dimension_semantics=sem, vmem_limit_bytes=VMEM_LIMIT)


def _norm_mod(x, g, scale, shift):
    ms = jnp.mean(x * x, axis=-1, keepdims=True)
    return (x * lax.rsqrt(ms + EPS) * g) * (1.0 + scale) + shift


def _ada_kernel(c_ref, w_ref, b_ref, o_ref):
    c = c_ref[...]
    ca = c * jax.nn.sigmoid(c)
    o_ref[0] = jnp.dot(ca, w_ref[0], preferred_element_type=F32, precision=HIGHEST) + b_ref[0]


def _ada(c, ada_w, ada_b):
    depth, d, n = ada_w.shape
    bsz = c.shape[0]
    tn = 1536 if n % 1536 == 0 else n
    return pl.pallas_call(
        _ada_kernel,
        grid=(depth, n // tn),
        in_specs=[
            pl.BlockSpec((bsz, d), lambda l, j: (0, 0)),
            pl.BlockSpec((1, d, tn), lambda l, j: (l, 0, j)),
            pl.BlockSpec((1, 1, tn), lambda l, j: (l, 0, j)),
        ],
        out_specs=pl.BlockSpec((1, bsz, tn), lambda l, j: (l, 0, j)),
        out_shape=jax.ShapeDtypeStruct((depth, bsz, n), F32),
        compiler_params=_cparams(("arbitrary", "arbitrary")),
        name="ada",
    )(c, ada_w, ada_b.reshape(depth, 1, n))


def _l0_kernel(x_ref, mod_ref, g_ref, w_ref, cw_ref, hgn_ref, lb_ref, y_ref,
               hb_scr, p_scr, q_scr, k_scr, v_scr, g_scr, b_scr, o_scr, st_scr,
               *, tm, layer):
    i = pl.program_id(1)
    cdim = cw_ref.shape[1]
    fdim = HG_HEADS * HG_DK

    @pl.when(i == 0)
    def _():
        st_scr[...] = jnp.zeros_like(st_scr)
        p_scr[0:SUBLANES, :] = jnp.zeros((SUBLANES, cdim), F32)

    mod = mod_ref[0, 0]
    h = _norm_mod(x_ref[0], g_ref[...], mod[1:2], mod[0:1])
    hb_scr[...] = h.astype(BF16)

    def proj(j, width):
        return jnp.dot(hb_scr[...], w_ref[:, j:j + width], preferred_element_type=F32)

    def conv_mixer():
        a_b = proj(0, cdim)
        p = proj(cdim, cdim) * proj(2 * cdim, cdim)
        p_scr[SUBLANES:SUBLANES + tm, :] = p
        p1 = p_scr[SUBLANES - 1:SUBLANES - 1 + tm, :]
        p2 = p_scr[SUBLANES - 2:SUBLANES - 2 + tm, :]
        cw = cw_ref[...]
        y_a = a_b * (cw[0:1] * p2 + cw[1:2] * p1 + cw[2:3] * p)
        p_scr[0:SUBLANES, :] = p_scr[tm:tm + SUBLANES, :]
        y_ref[0, :, 0:cdim] = y_a.astype(y_ref.dtype)

    base = 3 * cdim
    lbl = lb_ref[...]
    e = jnp.exp(lbl - jnp.max(lbl, axis=0, keepdims=True))
    lb = jnp.sum(e[0:layer + 1], axis=0, keepdims=True) / jnp.sum(e, axis=0, keepdims=True)
    q_scr[...] = proj(base, fdim)
    f = lb + (1.0 - lb) * jax.nn.sigmoid(proj(base + fdim, fdim))
    g_scr[...] = jnp.log(f)
    k_scr[...] = 1.0 - f
    v_scr[...] = proj(base + 2 * fdim, fdim)

    nchunk = tm // HG_CHUNK
    nsub = HG_CHUNK // HG_SUB
    nt = (((1,), (1,)), ((), ()))
    tn = (((0,), (0,)), ((), ()))
    ht = min(tm, HG_TILE)
    row = lax.broadcasted_iota(jnp.int32, (ht, ht), 0)
    col = lax.broadcasted_iota(jnp.int32, (ht, ht), 1)
    causal = (col <= row) & (col // HG_CHUNK == row // HG_CHUNK)
    for t0 in range(0, tm, ht):
        gl = g_scr[t0:t0 + ht, :]
        g_hi = gl.astype(BF16)
        r_mid = gl - g_hi.astype(F32)
        g_mid = r_mid.astype(BF16)
        g_lo = (r_mid - g_mid.astype(F32)).astype(BF16)
        cs = jnp.dot(causal.astype(BF16), jnp.concatenate([g_hi, g_mid, g_lo], axis=1),
                     preferred_element_type=F32)
        b_scr[t0:t0 + ht, :] = cs[:, 0:fdim] + cs[:, fdim:2 * fdim] + cs[:, 2 * fdim:3 * fdim]
    ends = jnp.concatenate([b_scr[(c + 1) * HG_CHUNK - 1:(c + 1) * HG_CHUNK, :] for c in range(nchunk)], axis=0)
    max_decay = jnp.max(-ends)

    def gated_output():
        o = o_scr[...]
        hgn = hgn_ref[...]
        parts = []
        for hd in range(HG_HEADS):
            ls = slice(hd * HG_DK, (hd + 1) * HG_DK)
            oh = o[:, ls]
            ms = jnp.mean(oh * oh, axis=-1, keepdims=True)
            parts.append(oh * lax.rsqrt(ms + EPS) * hgn[:, ls])
        hg = proj(base + 3 * fdim, fdim)
        y_b = jnp.concatenate(parts, axis=1) * (hg * jax.nn.sigmoid(hg))
        y_ref[0, :, cdim:cdim + fdim] = y_b.astype(y_ref.dtype)

    def fast_tile():
        conv_mixer()
        for t0 in range(0, tm, ht):
            ts = slice(t0, t0 + ht)
            bt = b_scr[ts, :]
            kt = k_scr[ts, :]
            q_hat = (q_scr[ts, :] * jnp.exp(bt)).astype(BF16)
            k_bar = (kt * jnp.exp(-bt)).astype(BF16)
            vb = v_scr[ts, :].astype(BF16)
            outs = []
            for hd in range(HG_HEADS):
                ls = slice(hd * HG_DK, (hd + 1) * HG_DK)
                sc = lax.dot_general(q_hat[:, ls], k_bar[:, ls], nt, preferred_element_type=F32)
                sc = jnp.where(causal, sc, 0.0).astype(BF16)
                intra = jnp.dot(sc, vb[:, ls], preferred_element_type=F32)
                st = st_scr[hd]
                inter = []
                for c in range(ht // HG_CHUNK):
                    rs = slice(c * HG_CHUNK, (c + 1) * HG_CHUNK)
                    inter.append(lax.dot_general(q_hat[rs, ls], st.astype(BF16), nt,
                                                 preferred_element_type=F32))
                    b_last = bt[(c + 1) * HG_CHUNK - 1:(c + 1) * HG_CHUNK, ls]
                    k_hat = (kt[rs, ls] * jnp.exp(b_last - bt[rs, ls])).astype(BF16)
                    st = st * jnp.exp(b_last) + lax.dot_general(vb[rs, ls], k_hat, tn,
                                                                preferred_element_type=F32)
                st_scr[hd] = st
                outs.append(intra + jnp.concatenate(inter, axis=0))
            o_scr[ts, :] = jnp.concatenate(outs, axis=1)
        gated_output()

    def safe_tile():
        conv_mixer()
        srow =lax.broadcasted_iota(jnp.int32, (HG_CHUNK, HG_CHUNK), 0)
        scol = lax.broadcasted_iota(jnp.int32, (HG_CHUNK, HG_CHUNK), 1)
        offmask = scol < (srow // HG_SUB) * HG_SUB
        sub_row = lax.broadcasted_iota(jnp.int32, (HG_SUB, 1), 0)

        def chunk_body(c, carry):
            r0 = pl.multiple_of(c * HG_CHUNK, HG_CHUNK)
            rows = pl.ds(r0, HG_CHUNK)
            qc = q_scr[rows, :]
            kc = k_scr[rows, :]
            bc = b_scr[rows, :]
            b_last = bc[HG_CHUNK - 1:HG_CHUNK]
            q_hat = (qc * jnp.exp(bc)).astype(BF16)
            k_hat = (kc * jnp.exp(b_last - bc)).astype(BF16)
            dec = jnp.exp(b_last)
            vb = v_scr[rows, :].astype(BF16)
            q_t, k_t = [], []
            for blk in range(1, nsub):
                lo = blk * HG_SUB
                ref = bc[lo - 1:lo]
                q_t.append((qc[lo:lo + HG_SUB] * jnp.exp(bc[lo:lo + HG_SUB] - ref)).astype(BF16))
                k_t.append((kc * jnp.exp(jnp.minimum(ref - bc, 0.0))).astype(BF16))
            outs = []
            for hd in range(HG_HEADS):
                ls = slice(hd * HG_DK, (hd + 1) * HG_DK)
                st = st_scr[hd]
                inter = lax.dot_general(q_hat[:, ls], st.astype(BF16), nt, preferred_element_type=F32)
                sc = [jnp.zeros((HG_SUB, HG_CHUNK), F32)]
                for blk in range(1, nsub):
                    sc.append(lax.dot_general(q_t[blk - 1][:, ls], k_t[blk - 1][:, ls], nt,
                                              preferred_element_type=F32))
                scores = jnp.where(offmask, jnp.concatenate(sc, axis=0), 0.0).astype(BF16)
                outs.append(inter + jnp.dot(scores, vb[:, ls], preferred_element_type=F32))
                upd = lax.dot_general(vb[:, ls], k_hat[:, ls], tn, preferred_element_type=F32)
                st_scr[hd] = st * dec[:, ls] + upd
            o_scr[rows, :] = jnp.concatenate(outs, axis=1)

            def diag_body(s, carry2):
                lo = pl.multiple_of((s // HG_SUB) * HG_SUB, HG_SUB)
                bs = b_scr[pl.ds(r0 + s, 1), :]
                ks = k_scr[pl.ds(r0 + s, 1), :]
                vs = v_scr[pl.ds(r0 + s, 1), :]
                qi_ = q_scr[pl.ds(r0 + lo, HG_SUB), :]
                bi_ = b_scr[pl.ds(r0 + lo, HG_SUB), :]
                a = qi_ * jnp.exp(jnp.minimum(bi_ - bs, 0.0)) * ks
                keep = sub_row >= s - lo
                parts = []
                for hd in range(HG_HEADS):
                    ls = slice(hd * HG_DK, (hd + 1) * HG_DK)
                    rsum = jnp.where(keep, jnp.sum(a[:, ls], axis=-1, keepdims=True), 0.0)
                    parts.append(rsum * vs[:, ls])
                orow = pl.ds(r0 + lo, HG_SUB)
                o_scr[orow, :] = o_scr[orow, :] + jnp.concatenate(parts, axis=1)
                return carry2

            lax.fori_loop(0, HG_CHUNK, diag_body, 0)
            return carry

        lax.fori_loop(0, nchunk, chunk_body, 0)
        gated_output()

    lax.cond(max_decay <= HG_FAST_MAX_EXP, fast_tile, safe_tile)


def _l0_mix(x, mods, g, w_in, conv_w, hg_norm, lb_logits, *, layer, tm):
    bsz, s, d = x.shape
    n_in = w_in.shape[1]
    cdim = conv_w.shape[1]
    fdim = HG_HEADS * HG_DK
    kern = functools.partial(_l0_kernel, tm=tm, layer=layer)
    return pl.pallas_call(
        kern,
        grid=(bsz, s // tm),
        in_specs=[
            pl.BlockSpec((1, tm, d), lambda b, i: (b, i, 0)),
            pl.BlockSpec((1, 1, 6, d), lambda b, i: (layer, b, 0, 0)),
            pl.BlockSpec((1, d), lambda b, i: (0, 0)),
            pl.BlockSpec((d, n_in), lambda b, i: (0, 0)),
            pl.BlockSpec(conv_w.shape, lambda b, i: (0, 0)),
            pl.BlockSpec((1, fdim), lambda b, i: (0, 0)),
            pl.BlockSpec(lb_logits.shape, lambda b, i: (0, 0)),
        ],
        out_specs=pl.BlockSpec((1, tm, cdim + fdim), lambda b, i: (b, i, 0)),
        out_shape=jax.ShapeDtypeStruct((bsz, s, cdim + fdim), BF16),
        scratch_shapes=[
            pltpu.VMEM((tm, d), BF16),
            pltpu.VMEM((tm + 2 * SUBLANES, cdim), F32),
            pltpu.VMEM((tm, fdim), F32),
            pltpu.VMEM((tm, fdim), F32),
            pltpu.VMEM((tm, fdim), F32),
            pltpu.VMEM((tm, fdim), F32),
            pltpu.VMEM((tm, fdim), F32),
            pltpu.VMEM((tm, fdim), F32),
            pltpu.VMEM((HG_HEADS, HG_DK, HG_DK), F32),
        ],
        compiler_params=_cparams(("arbitrary", "arbitrary")),
        name="l0_mix",
    )(x, mods, g.reshape(1, d), w_in, conv_w, hg_norm.reshape(1, fdim), lb_logits)


def _post_kernel(x_ref, y_ref, mod_ref, g_ref, wo_ref, w1_ref, w2_ref, o_ref, *, transposed, ff_chunk):
    mod = mod_ref[0, 0]
    if transposed:
        y = jnp.concatenate([y_ref[0, j].astype(F32).T for j in range(y_ref.shape[1])], axis=0).astype(BF16)
    else:
        y = y_ref[0]
    x1 = x_ref[0] + mod[2:3] * jnp.dot(y, wo_ref[...], preferred_element_type=F32)
    hb = _norm_mod(x1, g_ref[...], mod[4:5], mod[3:4]).astype(BF16)
    acc = jnp.zeros(x1.shape, F32)
    for c0 in range(0, w1_ref.shape[1], ff_chunk):
        a = jnp.dot(hb, w1_ref[:, c0:c0 + ff_chunk], preferred_element_type=F32)
        a = jnp.square(jnp.maximum(a, 0.0)).astype(BF16)
        acc = acc + jnp.dot(a, w2_ref[c0:c0 + ff_chunk, :], preferred_element_type=F32)
    o_ref[0] = x1 + mod[5:6] * acc


def _post(x, y, mods, g, w_out, w1, w2, *, layer, tm, transposed):
    bsz, s, d = x.shape
    dff = w1.shape[1]
    if transposed:
        blk = y.shape[3]
        y_spec = pl.BlockSpec((1, tm // blk, y.shape[2], blk), lambda b, i: (b, i, 0, 0))
    else:
        y_spec = pl.BlockSpec((1, tm, y.shape[2]), lambda b, i: (b, i, 0))
    kern = functools.partial(_post_kernel, transposed=transposed, ff_chunk=min(dff, 1024))
    return pl.pallas_call(
        kern,
        grid=(bsz, s // tm),
        in_specs=[
            pl.BlockSpec((1, tm, d), lambda b, i: (b, i, 0)),
            y_spec,
            pl.BlockSpec((1, 1, 6, d), lambda b, i: (layer, b, 0, 0)),
            pl.BlockSpec((1, d), lambda b, i: (0, 0)),
            pl.BlockSpec(w_out.shape, lambda b, i: (0, 0)),
            pl.BlockSpec(w1.shape, lambda b, i: (0, 0)),
            pl.BlockSpec(w2.shape, lambda b, i: (0, 0)),
        ],
        out_specs=pl.BlockSpec((1, tm, d), lambda b, i: (b, i, 0)),
        out_shape=jax.ShapeDtypeStruct((bsz, s, d), F32),
        compiler_params=_cparams(("arbitrary", "arbitrary")),
        name="post_t" if transposed else "post",
    )(x, y, mods, g.reshape(1, d), w_out, w1, w2)


def _qkv_kernel(x_ref, mod_ref, g_ref, wq_ref, wkv_ref, qn_ref, kn_ref, o_ref, h_scr, *, tm, q_scale):
    mod = mod_ref[0, 0]
    h = _norm_mod(x_ref[0], g_ref[...], mod[1:2], mod[0:1])
    nlane = h_scr.shape[2]
    for cc in range(h_scr.shape[0]):
        h_scr[cc] = h[:, cc * nlane:(cc + 1) * nlane]
    nt = (((1,), (1,)), ((), ()))
    width = SB_HEADS * SB_HEAD_DIM
    seg = ATT_BLOCK // SUBLANES

    def head_norm(t, gain, mult):
        parts = []
        for hd in range(SB_HEADS):
            th = t[hd * SB_HEAD_DIM:(hd + 1) * SB_HEAD_DIM]
            ms = jnp.mean(th * th, axis=0, keepdims=True)
            parts.append(th * lax.rsqrt(ms + EPS) * gain * mult)
        return jnp.concatenate(parts, axis=0)

    q_t = lax.dot_general(wq_ref[...], h.astype(BF16), nt, preferred_element_type=F32)
    q_t = head_norm(q_t, qn_ref[...], q_scale).astype(o_ref.dtype)
    cols = []
    for cc in range(h_scr.shape[0]):
        pieces = []
        for blk in range(tm // ATT_BLOCK):
            for j in range(seg):
                pieces.append(h_scr[cc, pl.ds(blk * ATT_BLOCK + j, SUBLANES, stride=seg), :])
        cols.append(jnp.concatenate(pieces, axis=0))
    hp = jnp.concatenate(cols, axis=1).astype(BF16)
    k_t = lax.dot_general(wkv_ref[0:width, :], hp, nt, preferred_element_type=F32)
    k_t = head_norm(k_t, kn_ref[...], 1.0).astype(o_ref.dtype)
    vraw_t = lax.dot_general(wkv_ref[width:2 * width, :], hp, nt, preferred_element_type=F32)
    pos = lax.broadcasted_iota(jnp.int32, (1, tm), 1)
    expo = ((pos % ATT_BLOCK) // SUBLANES) - seg
    v_scale = lax.bitcast_convert_type((expo + 127) << 23, F32)
    v_t = (vraw_t * v_scale).astype(o_ref.dtype)
    for blk in range(tm // ATT_BLOCK):
        ls = slice(blk * ATT_BLOCK, (blk + 1) * ATT_BLOCK)
        o_ref[0, blk, 0:width, :] = q_t[:, ls]
        o_ref[0, blk, width:2 * width, :] = k_t[:, ls]
        o_ref[0, blk, 2 * width:3 * width, :] = v_t[:, ls]


def _qkv(x, mods, g, wq_t, wkv_t, q_norm, k_norm, *, layer, tm):
    bsz, s, d = x.shape
    width = SB_HEADS * SB_HEAD_DIM
    q_scale = 0.5 * SB_HEAD_DIM ** -0.5
    kern = functools.partial(_qkv_kernel, tm=tm, q_scale=q_scale)
    return pl.pallas_call(
        kern,
        grid=(bsz, s // tm),
        in_specs=[
            pl.BlockSpec((1, tm, d), lambda b, i: (b, i, 0)),
            pl.BlockSpec((1, 1, 6, d), lambda b, i: (layer, b, 0, 0)),
            pl.BlockSpec((1, d), lambda b, i: (0, 0)),
            pl.BlockSpec(wq_t.shape, lambda b, i: (0, 0)),
            pl.BlockSpec(wkv_t.shape, lambda b, i: (0, 0)),
            pl.BlockSpec((SB_HEAD_DIM, 1), lambda b, i: (0, 0)),
            pl.BlockSpec((SB_HEAD_DIM, 1), lambda b, i: (0, 0)),
        ],
        out_specs=pl.BlockSpec((1, tm // ATT_BLOCK, 3 * width, ATT_BLOCK), lambda b, i: (b, i, 0, 0)),
        out_shape=jax.ShapeDtypeStruct((bsz, s // ATT_BLOCK, 3 * width, ATT_BLOCK), BF16),
        scratch_shapes=[pltpu.VMEM((d // LANES, tm, LANES), F32)],
        compiler_params=_cparams(("arbitrary", "arbitrary")),
        name="qkv",
    )(x, mods, g.reshape(1, d), wq_t, wkv_t, q_norm.reshape(SB_HEAD_DIM, 1), k_norm.reshape(SB_HEAD_DIM, 1))


def _attn_kernel(q_ref, k_ref, v_ref, o_ref, z_scr, w1_scr, w_scr, acc_scr, car_scr, *, heads):
    qi = pl.program_id(2)
    blk = ATT_BLOCK
    seg = blk // SUBLANES
    dh = SB_HEAD_DIM
    tn = (((0,), (0,)), ((), ()))
    acc_scr[...] = jnp.zeros_like(acc_scr)
    car_scr[...] = jnp.ones_like(car_scr)
    sub = lax.broadcasted_iota(jnp.int32, (SUBLANES, blk), 0)
    lane = lax.broadcasted_iota(jnp.int32, (SUBLANES, blk), 1)
    ones = jnp.ones((SUBLANES, blk), F32)

    def scores(g, kb):
        hs = slice(g * dh, (g + 1) * dh)
        return lax.dot_general(k_ref[0, kb, hs, :], q_ref[0, 0, hs, :], tn, preferred_element_type=F32)

    for g in range(ATT_QK_AHEAD):
        z_scr[g] = scores(g, qi)

    def process(kblk, masked):
        zs = {}
        for g in range(heads):
            hs = slice(g * dh, (g + 1) * dh)
            if g + ATT_QK_AHEAD < heads:
                zs[g + ATT_QK_AHEAD] = scores(g + ATT_QK_AHEAD, kblk)
            z = z_scr.at[g] if g < ATT_QK_AHEAD else zs.pop(g)
            if g == ATT_QK_AHEAD:
                for gn in range(ATT_QK_AHEAD):
                    z_scr[gn] = scores(gn, jnp.maximum(kblk - 1, 0))
            cs = ones
            for m in reversed(range(seg // 2)):
                pair = []
                for j in (2 * m + 1, 2 * m):
                    t = jnp.tanh(z[j * SUBLANES:(j + 1) * SUBLANES, :])
                    if masked:
                        t = jnp.where((sub * seg + j) < lane, t, -1.0)
                    x = cs * t
                    pair.append(cs + x)
                    cs = cs - x
                w1_scr[g, m * 2 * SUBLANES:(m + 1) * 2 * SUBLANES, :] = (
                    jnp.concatenate([pair[1], pair[0]], axis=0).astype(w1_scr.dtype))
            c = cs * (2.0 ** -seg)
            inc = c
            for d in (1, 2, 4):
                sh = pltpu.roll(inc, SUBLANES - d, axis=0)
                inc = inc * jnp.where(sub + d < SUBLANES, sh, 1.0)
            excl = jnp.where(sub + 1 < SUBLANES, pltpu.roll(inc, SUBLANES - 1, axis=0), 1.0)
            carry = car_scr[g]
            scale = excl * carry
            car_scr[g] = carry * jnp.broadcast_to(inc[0:1], (SUBLANES, blk))
            scale2 = jnp.concatenate([scale, scale], axis=0).astype(w_scr.dtype)
            for m in range(seg // 2):
                rows = slice(m * 2 * SUBLANES, (m + 1) * 2 * SUBLANES)
                w_scr[g, rows, :] = w1_scr[g, rows, :] * scale2
            acc_scr[hs, :] += jnp.dot(v_ref[0, kblk, hs, :], w_scr[g], preferred_element_type=F32)

    process(qi, True)

    def body(n, carry):
        process(qi - 1 - 2 * n, False)
        process(qi - 2 - 2 * n, False)
        return carry

    lax.fori_loop(0, qi // 2, body, 0)

    @pl.when(qi % 2 == 1)
    def _():
        process(0, False)
    o_ref[0, 0] = acc_scr[...].astype(o_ref.dtype)


def _attn(qkv_t, *, heads):
    bsz, ns, rows, blk = qkv_t.shape
    width = SB_HEADS * SB_HEAD_DIM
    ngroup = SB_HEADS // heads
    hrows = heads * SB_HEAD_DIM
    return pl.pallas_call(
        functools.partial(_attn_kernel, heads=heads),
        grid=(bsz, ngroup, ns),
        in_specs=[
            pl.BlockSpec((1, 1, hrows, blk), lambda b, h, i: (b, i, h, 0)),
            pl.BlockSpec((1, ns, hrows, blk), lambda b, h, i: (b, 0, ngroup + h, 0)),
            pl.BlockSpec((1, ns, hrows, blk), lambda b, h, i: (b, 0, 2 * ngroup + h, 0)),
        ],
        out_specs=pl.BlockSpec((1, 1, hrows, blk), lambda b, h, i: (b, i, h, 0)),
        out_shape=jax.ShapeDtypeStruct((bsz, ns, width, blk), BF16),
        scratch_shapes=[
            pltpu.VMEM((ATT_QK_AHEAD, blk, blk), F32),
            pltpu.VMEM((heads, blk, blk), BF16),
            pltpu.VMEM((heads, blk, blk), BF16),
            pltpu.VMEM((hrows, blk), F32),
            pltpu.VMEM((heads, SUBLANES, blk), F32),
        ],
        compiler_params=_cparams(("arbitrary", "arbitrary", "arbitrary")),
        name="attn",
    )(qkv_t, qkv_t, qkv_t)


def kernel(x, c, ada_w, ada_b, norm_mix, norm_mlp, w_in_ab, conv_w, hg_norm, lb_logits, w_out_ab,
           w_qkv, q_norm, k_norm, w_out_c, mlp_w1, mlp_w2):
    bsz, s, d = x.shape
    depth = ada_w.shape[0]
    tm = min(s, 512)
    width = SB_HEADS * SB_HEAD_DIM
    mods = _ada(c, ada_w, ada_b).reshape(depth, bsz, 6, d)
    for layer in range(depth):
        j = layer // 2
        w1 = mlp_w1[layer].astype(BF16)
        w2 = mlp_w2[layer].astype(BF16)
        if layer % 2 == 0:
            y = _l0_mix(x, mods, norm_mix[layer], w_in_ab[j].astype(BF16), conv_w[j], hg_norm[j],
                        lb_logits, layer=layer, tm=tm)
            x = _post(x, y, mods, norm_mlp[layer], w_out_ab[j].astype(BF16), w1, w2,
                      layer=layer, tm=tm, transposed=False)
        else:
            wq_t = w_qkv[j][:, 0:width].T.astype(BF16)
            wkv_t = w_qkv[j][:, width:3 * width].T.astype(BF16)
            qkv_t = _qkv(x, mods, norm_mix[layer], wq_t, wkv_t, q_norm[j], k_norm[j], layer=layer, tm=tm)
            o_t = _attn(qkv_t, heads=ATT_HEADS_PER_STEP)
            x = _post(x, o_t, mods, norm_mlp[layer], w_out_c[j].astype(BF16), w1, w2,
                      layer=layer, tm=tm, transposed=True)
    return x
```

```python
import functools

import jax
import jax.numpy as jnp
from jax import lax
from jax.experimental import pallas as pl
from jax.experimental.pallas import tpu as pltpu

F32 = jnp.float32
BF16 = jnp.bfloat16
HIGHEST = lax.Precision.HIGHEST

EPS = 1e-6
HG_CHUNK = 64
HG_SUB = 16
HG_FAST_MAX_EXP = 60.0
HG_TILE = 256
HG_HEADS = 4
HG_DK = 128
SB_HEADS = 16
SB_HEAD_DIM = 64
ATT_BLOCK = 256
ATT_HEADS_PER_STEP = 16
ATT_BLOCKS_PER_TRIP = 4
ATT_QK_AHEAD = 8
SUBLANES = 8
LANES = 128
VMEM_LIMIT = 56 * 1024 * 1024


def _cparams(sem):
    return pltpu.CompilerParams(dimension_semantics=sem, vmem_limit_bytes=VMEM_LIMIT)


def _norm_mod(x, g, scale, shift):
    ms = jnp.mean(x * x, axis=-1, keepdims=True)
    return (x * lax.rsqrt(ms + EPS) * g) * (1.0 + scale) + shift


def _ada_kernel(c_ref, w_ref, b_ref, o_ref):
    c = c_ref[...]
    ca = c * jax.nn.sigmoid(c)
    o_ref[0] = jnp.dot(ca, w_ref[0], preferred_element_type=F32, precision=HIGHEST) + b_ref[0]


def _ada(c, ada_w, ada_b):
    depth, d, n = ada_w.shape
    bsz = c.shape[0]
    tn = 1536 if n % 1536 == 0 else n
    return pl.pallas_call(
        _ada_kernel,
        grid=(depth, n // tn),
        in_specs=[
            pl.BlockSpec((bsz, d), lambda l, j: (0, 0)),
            pl.BlockSpec((1, d, tn), lambda l, j: (l, 0, j)),
            pl.BlockSpec((1, 1, tn), lambda l, j: (l, 0, j)),
        ],
        out_specs=pl.BlockSpec((1, bsz, tn), lambda l, j: (l, 0, j)),
        out_shape=jax.ShapeDtypeStruct((depth, bsz, n), F32),
        compiler_params=_cparams(("arbitrary", "arbitrary")),
        name="ada",
    )(c, ada_w, ada_b.reshape(depth, 1, n))


def _l0_kernel(x_ref, mod_ref, g_ref, w_ref, cw_ref, hgn_ref, lb_ref, y_ref,
               hb_scr, p_scr, q_scr, k_scr, v_scr, g_scr, b_scr, o_scr, st_scr,
               *, tm, layer):
    i = pl.program_id(1)
    cdim = cw_ref.shape[1]
    fdim = HG_HEADS * HG_DK

    @pl.when(i == 0)
    def _():
        st_scr[...] = jnp.zeros_like(st_scr)
        p_scr[0:SUBLANES, :] = jnp.zeros((SUBLANES, cdim), F32)

    mod = mod_ref[0, 0]
    h = _norm_mod(x_ref[0], g_ref[...], mod[1:2], mod[0:1])
    hb_scr[...] = h.astype(BF16)

    def proj(j, width):
        return jnp.dot(hb_scr[...], w_ref[:, j:j + width], preferred_element_type=F32)

    def conv_mixer():
        a_b = proj(0, cdim)
        p = proj(cdim, cdim) * proj(2 * cdim, cdim)
        p_scr[SUBLANES:SUBLANES + tm, :] = p
        p1 = p_scr[SUBLANES - 1:SUBLANES - 1 + tm, :]
        p2 = p_scr[SUBLANES - 2:SUBLANES - 2 + tm, :]
        cw = cw_ref[...]
        y_a = a_b * (cw[0:1] * p2 + cw[1:2] * p1 + cw[2:3] * p)
        p_scr[0:SUBLANES, :] = p_scr[tm:tm + SUBLANES, :]
        y_ref[0, :, 0:cdim] = y_a.astype(y_ref.dtype)

    base = 3 * cdim
    lbl = lb_ref[...]
    e = jnp.exp(lbl - jnp.max(lbl, axis=0, keepdims=True))
    lb = jnp.sum(e[0:layer + 1], axis=0, keepdims=True) / jnp.sum(e, axis=0, keepdims=True)
    q_scr[...] = proj(base, fdim)
    f = lb + (1.0 - lb) * jax.nn.sigmoid(proj(base + fdim, fdim))
    g_scr[...] = jnp.log(f)
    k_scr[...] = 1.0 - f
    v_scr[...] = proj(base + 2 * fdim, fdim)

    nchunk = tm // HG_CHUNK
    nsub = HG_CHUNK // HG_SUB
    nt = (((1,), (1,)), ((), ()))
    tn = (((0,), (0,)), ((), ()))
    ht = min(tm, HG_TILE)
    row = lax.broadcasted_iota(jnp.int32, (ht, ht), 0)
    col = lax.broadcasted_iota(jnp.int32, (ht, ht), 1)
    causal = (col <= row) & (col // HG_CHUNK == row // HG_CHUNK)
    for t0 in range(0, tm, ht):
        gl = g_scr[t0:t0 + ht, :]
        g_hi = gl.astype(BF16)
        r_mid = gl - g_hi.astype(F32)
        g_mid = r_mid.astype(BF16)
        g_lo = (r_mid - g_mid.astype(F32)).astype(BF16)
        cs = jnp.dot(causal.astype(BF16), jnp.concatenate([g_hi, g_mid, g_lo], axis=1),
                     preferred_element_type=F32)
        b_scr[t0:t0 + ht, :] = cs[:, 0:fdim] + cs[:, fdim:2 * fdim] + cs[:, 2 * fdim:3 * fdim]
    ends = jnp.concatenate([b_scr[(c + 1) * HG_CHUNK - 1:(c + 1) * HG_CHUNK, :] for c in range(nchunk)], axis=0)
    max_decay = jnp.max(-ends)

    def gated_output():
        o = o_scr[...]
        hgn = hgn_ref[...]
        parts = []
        for hd in range(HG_HEADS):
            ls = slice(hd * HG_DK, (hd + 1) * HG_DK)
            oh = o[:, ls]
            ms = jnp.mean(oh * oh, axis=-1, keepdims=True)
            parts.append(oh * lax.rsqrt(ms + EPS) * hgn[:, ls])
        hg = proj(base + 3 * fdim, fdim)
        y_b = jnp.concatenate(parts, axis=1) * (hg * jax.nn.sigmoid(hg))
        y_ref[0, :, cdim:cdim + fdim] = y_b.astype(y_ref.dtype)

    def fast_tile():
        conv_mixer()
        for t0 in range(0, tm, ht):
            ts = slice(t0, t0 + ht)
            bt = b_scr[ts, :]
            kt = k_scr[ts, :]
            q_hat = (q_scr[ts, :] * jnp.exp(bt)).astype(BF16)
            k_bar = (kt * jnp.exp(-bt)).astype(BF16)
            vb = v_scr[ts, :].astype(BF16)
            outs = []
            for hd in range(HG_HEADS):
                ls = slice(hd * HG_DK, (hd + 1) * HG_DK)
                sc = lax.dot_general(q_hat[:, ls], k_bar[:, ls], nt, preferred_element_type=F32)
                sc = jnp.where(causal, sc, 0.0).astype(BF16)
                intra = jnp.dot(sc, vb[:, ls], preferred_element_type=F32)
                st = st_scr[hd]
                inter = []
                for c in range(ht // HG_CHUNK):
                    rs = slice(c * HG_CHUNK, (c + 1) * HG_CHUNK)
                    inter.append(lax.dot_general(q_hat[rs, ls], st.astype(BF16), nt,
                                                 preferred_element_type=F32))
                    b_last = bt[(c + 1) * HG_CHUNK - 1:(c + 1) * HG_CHUNK, ls]
                    k_hat = (kt[rs, ls] * jnp.exp(b_last - bt[rs, ls])).astype(BF16)
                    st = st * jnp.exp(b_last) + lax.dot_general(vb[rs, ls], k_hat, tn,
                                                                preferred_element_type=F32)
                st_scr[hd] = st
                outs.append(intra + jnp.concatenate(inter, axis=0))
            o_scr[ts, :] = jnp.concatenate(outs, axis=1)
        gated_output()

    def safe_tile():
        conv_mixer()
        srow =lax.broadcasted_iota(jnp.int32, (HG_CHUNK, HG_CHUNK), 0)
        scol = lax.broadcasted_iota(jnp.int32, (HG_CHUNK, HG_CHUNK), 1)
        offmask = scol < (srow // HG_SUB) * HG_SUB
        sub_row = lax.broadcasted_iota(jnp.int32, (HG_SUB, 1), 0)

        def chunk_body(c, carry):
            r0 = pl.multiple_of(c * HG_CHUNK, HG_CHUNK)
            rows = pl.ds(r0, HG_CHUNK)
            qc = q_scr[rows, :]
            kc = k_scr[rows, :]
            bc = b_scr[rows, :]
            b_last = bc[HG_CHUNK - 1:HG_CHUNK]
            q_hat = (qc * jnp.exp(bc)).astype(BF16)
            k_hat = (kc * jnp.exp(b_last - bc)).astype(BF16)
            dec = jnp.exp(b_last)
            vb = v_scr[rows, :].astype(BF16)
            q_t, k_t = [], []
            for blk in range(1, nsub):
                lo = blk * HG_SUB
                ref = bc[lo - 1:lo]
                q_t.append((qc[lo:lo + HG_SUB] * jnp.exp(bc[lo:lo + HG_SUB] - ref)).astype(BF16))
                k_t.append((kc * jnp.exp(jnp.minimum(ref - bc, 0.0))).astype(BF16))
            outs = []
            for hd in range(HG_HEADS):
                ls = slice(hd * HG_DK, (hd + 1) * HG_DK)
                st = st_scr[hd]
                inter = lax.dot_general(q_hat[:, ls], st.astype(BF16), nt, preferred_element_type=F32)
                sc = [jnp.zeros((HG_SUB, HG_CHUNK), F32)]
                for blk in range(1, nsub):
                    sc.append(lax.dot_general(q_t[blk - 1][:, ls], k_t[blk - 1][:, ls], nt,
                                              preferred_element_type=F32))
                scores = jnp.where(offmask, jnp.concatenate(sc, axis=0), 0.0).astype(BF16)
                outs.append(inter + jnp.dot(scores, vb[:, ls], preferred_element_type=F32))
                upd = lax.dot_general(vb[:, ls], k_hat[:, ls], tn, preferred_element_type=F32)
                st_scr[hd] = st * dec[:, ls] + upd
            o_scr[rows, :] = jnp.concatenate(outs, axis=1)

            def diag_body(s, carry2):
                lo = pl.multiple_of((s // HG_SUB) * HG_SUB, HG_SUB)
                bs = b_scr[pl.ds(r0 + s, 1), :]
                ks = k_scr[pl.ds(r0 + s, 1), :]
                vs = v_scr[pl.ds(r0 + s, 1), :]
                qi_ = q_scr[pl.ds(r0 + lo, HG_SUB), :]
                bi_ = b_scr[pl.ds(r0 + lo, HG_SUB), :]
                a = qi_ * jnp.exp(jnp.minimum(bi_ - bs, 0.0)) * ks
                keep = sub_row >= s - lo
                parts = []
                for hd in range(HG_HEADS):
                    ls = slice(hd * HG_DK, (hd + 1) * HG_DK)
                    rsum = jnp.where(keep, jnp.sum(a[:, ls], axis=-1, keepdims=True), 0.0)
                    parts.append(rsum * vs[:, ls])
                orow = pl.ds(r0 + lo, HG_SUB)
                o_scr[orow, :] = o_scr[orow, :] + jnp.concatenate(parts, axis=1)
                return carry2

            lax.fori_loop(0, HG_CHUNK, diag_body, 0)
            return carry

        lax.fori_loop(0, nchunk, chunk_body, 0)
        gated_output()

    lax.cond(max_decay <= HG_FAST_MAX_EXP, fast_tile, safe_tile)


def _l0_mix(x, mods, g, w_in, conv_w, hg_norm, lb_logits, *, layer, tm):
    bsz, s, d = x.shape
    n_in = w_in.shape[1]
    cdim = conv_w.shape[1]
    fdim = HG_HEADS * HG_DK
    kern = functools.partial(_l0_kernel, tm=tm, layer=layer)
    return pl.pallas_call(
        kern,
        grid=(bsz, s // tm),
        in_specs=[
            pl.BlockSpec((1, tm, d), lambda b, i: (b, i, 0)),
            pl.BlockSpec((1, 1, 6, d), lambda b, i: (layer, b, 0, 0)),
            pl.BlockSpec((1, d), lambda b, i: (0, 0)),
            pl.BlockSpec((d, n_in), lambda b, i: (0, 0)),
            pl.BlockSpec(conv_w.shape, lambda b, i: (0, 0)),
            pl.BlockSpec((1, fdim), lambda b, i: (0, 0)),
            pl.BlockSpec(lb_logits.shape, lambda b, i: (0, 0)),
        ],
        out_specs=pl.BlockSpec((1, tm, cdim + fdim), lambda b, i: (b, i, 0)),
        out_shape=jax.ShapeDtypeStruct((bsz, s, cdim + fdim), BF16),
        scratch_shapes=[
            pltpu.VMEM((tm, d), BF16),
            pltpu.VMEM((tm + 2 * SUBLANES, cdim), F32),
            pltpu.VMEM((tm, fdim), F32),
            pltpu.VMEM((tm, fdim), F32),
            pltpu.VMEM((tm, fdim), F32),
            pltpu.VMEM((tm, fdim), F32),
            pltpu.VMEM((tm, fdim), F32),
            pltpu.VMEM((tm, fdim), F32),
            pltpu.VMEM((HG_HEADS, HG_DK, HG_DK), F32),
        ],
        compiler_params=_cparams(("arbitrary", "arbitrary")),
        name="l0_mix",
    )(x, mods, g.reshape(1, d), w_in, conv_w, hg_norm.reshape(1, fdim), lb_logits)


def _post_kernel(x_ref, y_ref, mod_ref, g_ref, wo_ref, w1_ref, w2_ref, o_ref, *, transposed, ff_chunk):
    mod = mod_ref[0, 0]
    if transposed:
        y = jnp.concatenate([y_ref[0, j].astype(F32).T for j in range(y_ref.shape[1])], axis=0).astype(BF16)
    else:
        y = y_ref[0]
    x1 = x_ref[0] + mod[2:3] * jnp.dot(y, wo_ref[...], preferred_element_type=F32)
    hb = _norm_mod(x1, g_ref[...], mod[4:5], mod[3:4]).astype(BF16)
    acc = jnp.zeros(x1.shape, F32)
    for c0 in range(0, w1_ref.shape[1], ff_chunk):
        a = jnp.dot(hb, w1_ref[:, c0:c0 + ff_chunk], preferred_element_type=F32)
        a = jnp.square(jnp.maximum(a, 0.0)).astype(BF16)
        acc = acc + jnp.dot(a, w2_ref[c0:c0 + ff_chunk, :], preferred_element_type=F32)
    o_ref[0] = x1 + mod[5:6] * acc


def _post(x, y, mods, g, w_out, w1, w2, *, layer, tm, transposed):
    bsz, s, d = x.shape
    dff = w1.shape[1]
    if transposed:
        blk = y.shape[3]
        y_spec = pl.BlockSpec((1, tm // blk, y.shape[2], blk), lambda b, i: (b, i, 0, 0))
    else:
        y_spec = pl.BlockSpec((1, tm, y.shape[2]), lambda b, i: (b, i, 0))
    kern = functools.partial(_post_kernel, transposed=transposed, ff_chunk=min(dff, 1024))
    return pl.pallas_call(
        kern,
        grid=(bsz, s // tm),
        in_specs=[
            pl.BlockSpec((1, tm, d), lambda b, i: (b, i, 0)),
            y_spec,
            pl.BlockSpec((1, 1, 6, d), lambda b, i: (layer, b, 0, 0)),
            pl.BlockSpec((1, d), lambda b, i: (0, 0)),
            pl.BlockSpec(w_out.shape, lambda b, i: (0, 0)),
            pl.BlockSpec(w1.shape, lambda b, i: (0, 0)),
            pl.BlockSpec(w2.shape, lambda b, i: (0, 0)),
        ],
        out_specs=pl.BlockSpec((1, tm, d), lambda b, i: (b, i, 0)),
        out_shape=jax.ShapeDtypeStruct((bsz, s, d), F32),
        compiler_params=_cparams(("arbitrary", "arbitrary")),
        name="post_t" if transposed else "post",
    )(x, y, mods, g.reshape(1, d), w_out, w1, w2)


def _qkv_kernel(x_ref, mod_ref, g_ref, wq_ref, wkv_ref, qn_ref, kn_ref, o_ref, h_scr, *, tm, q_scale):
    mod = mod_ref[0, 0]
    h = _norm_mod(x_ref[0], g_ref[...], mod[1:2], mod[0:1])
    nlane = h_scr.shape[2]
    for cc in range(h_scr.shape[0]):
        h_scr[cc] = h[:, cc * nlane:(cc + 1) * nlane]
    nt = (((1,), (1,)), ((), ()))
    width = SB_HEADS * SB_HEAD_DIM
    seg = ATT_BLOCK // SUBLANES

    def head_norm(t, gain, mult):
        parts = []
        for hd in range(SB_HEADS):
            th = t[hd * SB_HEAD_DIM:(hd + 1) * SB_HEAD_DIM]
            ms = jnp.mean(th * th, axis=0, keepdims=True)
            parts.append(th * lax.rsqrt(ms + EPS) * gain * mult)
        return jnp.concatenate(parts, axis=0)

    q_t = lax.dot_general(wq_ref[...], h.astype(BF16), nt, preferred_element_type=F32)
    q_t = head_norm(q_t, qn_ref[...], q_scale).astype(o_ref.dtype)
    cols = []
    for cc in range(h_scr.shape[0]):
        pieces = []
        for blk in range(tm // ATT_BLOCK):
            for j in range(seg):
                pieces.append(h_scr[cc, pl.ds(blk * ATT_BLOCK + j, SUBLANES, stride=seg), :])
        cols.append(jnp.concatenate(pieces, axis=0))
    hp = jnp.concatenate(cols, axis=1).astype(BF16)
    k_t = lax.dot_general(wkv_ref[0:width, :], hp, nt, preferred_element_type=F32)
    k_t = head_norm(k_t, kn_ref[...], 1.0).astype(o_ref.dtype)
    vraw_t = lax.dot_general(wkv_ref[width:2 * width, :], hp, nt, preferred_element_type=F32)
    pos = lax.broadcasted_iota(jnp.int32, (1, tm), 1)
    expo = ((pos % ATT_BLOCK) // SUBLANES) - seg
    v_scale = lax.bitcast_convert_type((expo + 127) << 23, F32)
    v_t = (vraw_t * v_scale).astype(o_ref.dtype)
    for blk in range(tm // ATT_BLOCK):
        ls = slice(blk * ATT_BLOCK, (blk + 1) * ATT_BLOCK)
        o_ref[0, blk, 0:width, :] = q_t[:, ls]
        o_ref[0, blk, width:2 * width, :] = k_t[:, ls]
        o_ref[0, blk, 2 * width:3 * width, :] = v_t[:, ls]


def _qkv(x, mods, g, wq_t, wkv_t, q_norm, k_norm, *, layer, tm):
    bsz, s, d = x.shape
    width = SB_HEADS * SB_HEAD_DIM
    q_scale = 0.5 * SB_HEAD_DIM ** -0.5
    kern = functools.partial(_qkv_kernel, tm=tm, q_scale=q_scale)
    return pl.pallas_call(
        kern,
        grid=(bsz, s // tm),
        in_specs=[
            pl.BlockSpec((1, tm, d), lambda b, i: (b, i, 0)),
            pl.BlockSpec((1, 1, 6, d), lambda b, i: (layer, b, 0, 0)),
            pl.BlockSpec((1, d), lambda b, i: (0, 0)),
            pl.BlockSpec(wq_t.shape, lambda b, i: (0, 0)),
            pl.BlockSpec(wkv_t.shape, lambda b, i: (0, 0)),
            pl.BlockSpec((SB_HEAD_DIM, 1), lambda b, i: (0, 0)),
            pl.BlockSpec((SB_HEAD_DIM, 1), lambda b, i: (0, 0)),
        ],
        out_specs=pl.BlockSpec((1, tm // ATT_BLOCK, 3 * width, ATT_BLOCK), lambda b, i: (b, i, 0, 0)),
        out_shape=jax.ShapeDtypeStruct((bsz, s // ATT_BLOCK, 3 * width, ATT_BLOCK), BF16),
        scratch_shapes=[pltpu.VMEM((d // LANES, tm, LANES), F32)],
        compiler_params=_cparams(("arbitrary", "arbitrary")),
        name="qkv",
    )(x, mods, g.reshape(1, d), wq_t, wkv_t, q_norm.reshape(SB_HEAD_DIM, 1), k_norm.reshape(SB_HEAD_DIM, 1))


def _attn_kernel(q_ref, k_ref, v_ref, o_ref, z_scr, w1_scr, w_scr, acc_scr, car_scr, *, heads):
    qi = pl.program_id(2)
    blk = ATT_BLOCK
    seg = blk // SUBLANES
    dh = SB_HEAD_DIM
    tn = (((0,), (0,)), ((), ()))
    acc_scr[...] = jnp.zeros_like(acc_scr)
    car_scr[...] = jnp.ones_like(car_scr)
    sub = lax.broadcasted_iota(jnp.int32, (SUBLANES, blk), 0)
    lane = lax.broadcasted_iota(jnp.int32, (SUBLANES, blk), 1)
    ones = jnp.ones((SUBLANES, blk), F32)

    def scores(g, kb):
        hs = slice(g * dh, (g + 1) * dh)
        return lax.dot_general(k_ref[0, kb, hs, :], q_ref[0, 0, hs, :], tn, preferred_element_type=F32)

    for g in range(ATT_QK_AHEAD):
        z_scr[g] = scores(g, qi)

    def process(kblk, masked):
        zs = {}
        for g in range(heads):
            hs = slice(g * dh, (g + 1) * dh)
            if g + ATT_QK_AHEAD < heads:
                zs[g + ATT_QK_AHEAD] = scores(g + ATT_QK_AHEAD, kblk)
            z = z_scr.at[g] if g < ATT_QK_AHEAD else zs.pop(g)
            if g == ATT_QK_AHEAD:
                for gn in range(ATT_QK_AHEAD):
                    z_scr[gn] = scores(gn, jnp.maximum(kblk - 1, 0))
            cs = ones
            for m in reversed(range(seg // 2)):
                pair = []
                for j in (2 * m + 1, 2 * m):
                    t = jnp.tanh(z[j * SUBLANES:(j + 1) * SUBLANES, :])
                    if masked:
                        t = jnp.where((sub * seg + j) < lane, t, -1.0)
                    x = cs * t
                    pair.append(cs + x)
                    cs = cs - x
                w1_scr[g, m * 2 * SUBLANES:(m + 1) * 2 * SUBLANES, :] = (
                    jnp.concatenate([pair[1], pair[0]], axis=0).astype(w1_scr.dtype))
            c = cs * (2.0 ** -seg)
            inc = c
            for d in (1, 2, 4):
                sh = pltpu.roll(inc, SUBLANES - d, axis=0)
                inc = inc * jnp.where(sub + d < SUBLANES, sh, 1.0)
            excl = jnp.where(sub + 1 < SUBLANES, pltpu.roll(inc, SUBLANES - 1, axis=0), 1.0)
            carry = car_scr[g]
            scale = excl * carry
            car_scr[g] = carry * jnp.broadcast_to(inc[0:1], (SUBLANES, blk))
            scale2 = jnp.concatenate([scale, scale], axis=0).astype(w_scr.dtype)
            for m in range(seg // 2):
                rows = slice(m * 2 * SUBLANES, (m + 1) * 2 * SUBLANES)
                w_scr[g, rows, :] = w1_scr[g, rows, :] * scale2
            acc_scr[hs, :] += jnp.dot(v_ref[0, kblk, hs, :], w_scr[g], preferred_element_type=F32)

    process(qi, True)

    def body(n, carry):
        for u in range(ATT_BLOCKS_PER_TRIP):
            process(qi - 1 - ATT_BLOCKS_PER_TRIP * n - u, False)
        return carry

    lax.fori_loop(0, qi // ATT_BLOCKS_PER_TRIP, body, 0)
    rem = qi % ATT_BLOCKS_PER_TRIP

    @pl.when(rem >= 2)
    def _():
        process(rem - 1, False)
        process(rem - 2, False)

    @pl.when(rem % 2 == 1)
    def _():
        process(0, False)
    o_ref[0, 0] = acc_scr[...].astype(o_ref.dtype)


def _attn(qkv_t, *, heads):
    bsz, ns, rows, blk = qkv_t.shape
    width = SB_HEADS * SB_HEAD_DIM
    ngroup = SB_HEADS // heads
    hrows = heads * SB_HEAD_DIM
    return pl.pallas_call(
        functools.partial(_attn_kernel, heads=heads),
        grid=(bsz, ngroup, ns),
        in_specs=[
            pl.BlockSpec((1, 1, hrows, blk), lambda b, h, i: (b, i, h, 0)),
            pl.BlockSpec((1, ns, hrows, blk), lambda b, h, i: (b, 0, ngroup + h, 0)),
            pl.BlockSpec((1, ns, hrows, blk), lambda b, h, i: (b, 0, 2 * ngroup + h, 0)),
        ],
        out_specs=pl.BlockSpec((1, 1, hrows, blk), lambda b, h, i: (b, i, h, 0)),
        out_shape=jax.ShapeDtypeStruct((bsz, ns, width, blk), BF16),
        scratch_shapes=[
            pltpu.VMEM((ATT_QK_AHEAD, blk, blk), F32),
            pltpu.VMEM((heads, blk, blk), BF16),
            pltpu.VMEM((heads, blk, blk), BF16),
            pltpu.VMEM((hrows, blk), F32),
            pltpu.VMEM((heads, SUBLANES, blk), F32),
        ],
        compiler_params=_cparams(("arbitrary", "arbitrary", "arbitrary")),
        name="attn",
    )(qkv_t, qkv_t, qkv_t)


def kernel(x, c, ada_w, ada_b, norm_mix, norm_mlp, w_in_ab, conv_w, hg_norm, lb_logits, w_out_ab,
           w_qkv, q_norm, k_norm, w_out_c, mlp_w1, mlp_w2):
    bsz, s, d = x.shape
    depth = ada_w.shape[0]
    tm = min(s, 512)
    width = SB_HEADS * SB_HEAD_DIM
    mods = _ada(c, ada_w, ada_b).reshape(depth, bsz, 6, d)
    for layer in range(depth):
        j = layer // 2
        w1 = mlp_w1[layer].astype(BF16)
        w2 = mlp_w2[layer].astype(BF16)
        if layer % 2 == 0:
            y = _l0_mix(x, mods, norm_mix[layer], w_in_ab[j].astype(BF16), conv_w[j], hg_norm[j],
                        lb_logits, layer=layer, tm=tm)
            x = _post(x, y, mods, norm_mlp[layer], w_out_ab[j].astype(BF16), w1, w2,
                      layer=layer, tm=tm, transposed=False)
        else:
            wq_t = w_qkv[j][:, 0:width].T.astype(BF16)
            wkv_t = w_qkv[j][:, width:3 * width].T.astype(BF16)
            qkv_t = _qkv(x, mods, norm_mix[layer], wq_t, wkv_t, q_norm[j], k_norm[j], layer=layer, tm=tm)
            o_t = _attn(qkv_t, heads=ATT_HEADS_PER_STEP)
            x = _post(x, o_t, mods, norm_mlp[layer], w_out_c[j].astype(BF16), w1, w2,
                      layer=layer, tm=tm, transposed=True)
    return x
```
